```python
import jax, jax.numpy as jnp
from jax import lax
import numpy as np

D_MODEL = 4096
BATCH = 1
SEQ = 8192
DEPTH = 1

PLE_DIM = 256
HEAD_DIM = 64
ATTN_WIDTH = D_MODEL // 2
N_Q_HEADS = ATTN_WIDTH // HEAD_DIM
GQA_GROUP = 8
N_KV_HEADS = N_Q_HEADS // GQA_GROUP
KV_WIDTH = N_KV_HEADS * HEAD_DIM
WINDOW = 128
BLOCK = 128
POOL_WINDOWS = (2, 4, 8, 16)
N_POOL_GROUPS = 4
POOL_WIDTH = D_MODEL // 2
POOL_GROUP_DIM = POOL_WIDTH // N_POOL_GROUPS
IN_WIDTH = ATTN_WIDTH + 2 * KV_WIDTH + POOL_WIDTH + 2 * D_MODEL
D_FF = 11008
CONV_WIDTH = 3
RMS_EPS = 1e-6
MASK_VALUE = -1e30

kernel_name = "hybrid_swa_sink_pool_convffn_ple"


def rms_norm(x, gain):
    xf = x.astype(jnp.float32)
    y = xf * lax.rsqrt(jnp.mean(xf * xf, axis=-1, keepdims=True) + RMS_EPS)
    return (y * gain.astype(jnp.float32)).astype(x.dtype)


def sliding_window_attention(q, k, v, sinks):
    b, s, _ = q.shape
    nb = s // BLOCK
    qb = q.reshape(b, nb, BLOCK, N_KV_HEADS, GQA_GROUP, HEAD_DIM)

    def band(t):
        t = t.reshape(b, s, N_KV_HEADS, HEAD_DIM)
        t = jnp.pad(t, ((0, 0), (BLOCK, 0), (0, 0), (0, 0)))
        t = t.reshape(b, nb + 1, BLOCK, N_KV_HEADS, HEAD_DIM)
        return jnp.concatenate([t[:, :-1], t[:, 1:]], axis=2)

    kb, vb = band(k), band(v)
    scores = jnp.einsum('bnqhgd,bnkhd->bnhgqk', qb, kb).astype(jnp.float32) * (HEAD_DIM ** -0.5)
    qi = jnp.arange(BLOCK)[:, None]
    kj = jnp.arange(2 * BLOCK)[None, :]
    rel = kj - BLOCK - qi
    in_band = (rel <= 0) & (rel > -WINDOW)
    key_pos = jnp.arange(nb)[:, None, None] * BLOCK + kj[None] - BLOCK
    mask = in_band[None] & (key_pos >= 0)
    scores = jnp.where(mask[None, :, None, None], scores, MASK_VALUE)
    sink = jnp.broadcast_to(sinks.astype(jnp.float32).reshape(1, 1, N_KV_HEADS, GQA_GROUP, 1, 1),
                            scores.shape[:-1] + (1,))
    probs = jax.nn.softmax(jnp.concatenate([scores, sink], axis=-1), axis=-1)[..., :-1]
    out = jnp.einsum('bnhgqk,bnkhd->bnqhgd', probs.astype(v.dtype), vb)
    return out.reshape(b, s, ATTN_WIDTH)


def multiscale_pool(u, w_pool, pool_scale):
    b, s, _ = u.shape
    ug = u.reshape(b, s, N_POOL_GROUPS, POOL_GROUP_DIM)
    cs = jnp.cumsum(ug.astype(jnp.float32), axis=1)
    t = jnp.arange(1, s + 1, dtype=jnp.float32)
    means = []
    for gi, w in enumerate(POOL_WINDOWS):
        c = cs[:, :, gi]
        prev = jnp.pad(c, ((0, 0), (w, 0), (0, 0)))[:, :s]
        count = jnp.minimum(t, float(w))[None, :, None]
        means.append((c - prev) / count)
    mixed = (jnp.stack(means, axis=2) - ug.astype(jnp.float32)).astype(u.dtype)
    y = jnp.einsum('bsgc,gcd->bsgd', mixed, w_pool).reshape(b, s, POOL_WIDTH)
    return y * pool_scale


def causal_depthwise_conv(x, w, bias):
    y = lax.conv_general_dilated(
        x, w[:, None, :], window_strides=(1,), padding=((CONV_WIDTH - 1, 0),),
        dimension_numbers=('NWC', 'WIO', 'NWC'), feature_group_count=x.shape[-1])
    return y + bias


def setup_inputs(seed: int = 0) -> dict:
    key = jax.random.key(seed)
    ks = jax.random.split(key, 24)
    f32 = jnp.float32

    def w(k, shape, fan_in):
        return jax.random.normal(k, shape, f32) * (fan_in ** -0.5)

    def gain(k, n):
        return 1.0 + 0.05 * jax.random.normal(k, (DEPTH, n), f32)

    return {
        "x": jax.random.normal(ks[0], (BATCH, SEQ, D_MODEL), f32),
        "p": jax.random.normal(ks[1], (DEPTH, BATCH, SEQ, PLE_DIM), f32),
        "norm_mix_pre": gain(ks[2], D_MODEL),
        "w_in": w(ks[3], (DEPTH, D_MODEL, IN_WIDTH), D_MODEL),
        "attn_sinks": jax.random.normal(ks[4], (DEPTH, N_Q_HEADS), f32),
        "w_pool": w(ks[5], (DEPTH, N_POOL_GROUPS, POOL_GROUP_DIM, POOL_GROUP_DIM), POOL_GROUP_DIM),
        "pool_scale": 1.0 + 0.1 * jax.random.normal(ks[6], (DEPTH, POOL_WIDTH), f32),
        "w_branch_attn": w(ks[7], (DEPTH, ATTN_WIDTH, D_MODEL), ATTN_WIDTH),
        "w_branch_pool": w(ks[8], (DEPTH, POOL_WIDTH, D_MODEL), POOL_WIDTH),
        "w_out": w(ks[9], (DEPTH, D_MODEL, D_MODEL), D_MODEL),
        "norm_mix_post": gain(ks[10], D_MODEL),
        "norm_ffn_pre": gain(ks[11], D_MODEL),
        "w_up": w(ks[12], (DEPTH, D_MODEL, 2 * D_FF), D_MODEL),
        "conv_w": w(ks[13], (DEPTH, CONV_WIDTH, 2 * D_FF), CONV_WIDTH),
        "conv_b": 0.01 * jax.random.normal(ks[14], (DEPTH, 2 * D_FF), f32),
        "w_down": w(ks[15], (DEPTH, D_FF, D_MODEL), D_FF),
        "norm_ffn_post": gain(ks[16], D_MODEL),
        "norm_ple_gate": gain(ks[17], D_MODEL),
        "w_ple_gate": w(ks[18], (DEPTH, D_MODEL, D_MODEL), D_MODEL),
        "w_ple": w(ks[19], (DEPTH, PLE_DIM, D_MODEL), PLE_DIM),
        "norm_ple_post": gain(ks[20], D_MODEL),
    }


def reference(x, p, norm_mix_pre, w_in, attn_sinks, w_pool, pool_scale, w_branch_attn,
              w_branch_pool, w_out, norm_mix_post, norm_ffn_pre, w_up, conv_w, conv_b,
              w_down, norm_ffn_post, norm_ple_gate, w_ple_gate, w_ple, norm_ple_post):
    split_points = [ATTN_WIDTH, ATTN_WIDTH + KV_WIDTH, ATTN_WIDTH + 2 * KV_WIDTH,
                    ATTN_WIDTH + 2 * KV_WIDTH + POOL_WIDTH,
                    ATTN_WIDTH + 2 * KV_WIDTH + POOL_WIDTH + D_MODEL]
    for i in range(DEPTH):
        h = rms_norm(x, norm_mix_pre[i])
        z = h @ w_in[i]
        q, k, v, u, g_attn, g_pool = jnp.split(z, split_points, axis=-1)
        y_attn = sliding_window_attention(q, k, v, attn_sinks[i]) @ w_branch_attn[i]
        y_pool = multiscale_pool(u, w_pool[i], pool_scale[i]) @ w_branch_pool[i]
        merged = jax.nn.sigmoid(g_attn) * y_attn + jax.nn.sigmoid(g_pool) * y_pool
        x = x + rms_norm(merged @ w_out[i], norm_mix_post[i])
        h = rms_norm(x, norm_ffn_pre[i])
        up = causal_depthwise_conv(h @ w_up[i], conv_w[i], conv_b[i])
        gate, val = jnp.split(up, 2, axis=-1)
        x = x + rms_norm((jax.nn.gelu(gate, approximate=True) * val) @ w_down[i], norm_ffn_post[i])
        ple_gate = jax.nn.sigmoid(rms_norm(x, norm_ple_gate[i]) @ w_ple_gate[i])
        x = x + rms_norm(ple_gate * (p[i] @ w_ple[i]), norm_ple_post[i])
    return x
```

```python
import functools

import jax
import jax.numpy as jnp
from jax import lax
from jax.experimental import pallas as pl
from jax.experimental.pallas import tpu as pltpu

F32 = jnp.float32
BF16 = jnp.bfloat16

HEAD_DIM = 64
GQA_GROUP = 8
N_KV_HEADS = 4
ATTN_BLOCK = 128
POOL_WINDOWS = (2, 4, 8, 16)
POOL_HALO = 16
CONV_WIDTH = 3
RMS_EPS = 1e-6
MASK_VALUE = -1e30

LANES = 128
SUBLANES = 8
VMEM_LIMIT_BYTES = 60 * 1024 * 1024


def _params(n_grid_dims):
    return pltpu.CompilerParams(
        dimension_semantics=("arbitrary",) * n_grid_dims,
        vmem_limit_bytes=VMEM_LIMIT_BYTES)


def _rms(x, gain):
    ms = jnp.mean(x * x, axis=-1, keepdims=True)
    return x * lax.rsqrt(ms + RMS_EPS) * gain


def _norm_kernel(x_ref, g_ref, o_ref):
    o_ref[...] = _rms(x_ref[...], g_ref[...]).astype(o_ref.dtype)


def rmsnorm_cast(x, gain, *, bm=256):
    s, d = x.shape
    return pl.pallas_call(
        _norm_kernel,
        grid=(s // bm,),
        in_specs=[pl.BlockSpec((bm, d), lambda i: (i, 0)),
                  pl.BlockSpec((1, d), lambda i: (0, 0))],
        out_specs=pl.BlockSpec((bm, d), lambda i: (i, 0)),
        out_shape=jax.ShapeDtypeStruct((s, d), BF16),
        compiler_params=_params(1),
        name="rmsnorm_cast",
    )(x, gain.reshape(1, d))


def _resnorm_kernel(x_ref, t_ref, gp_ref, gn_ref, xo_ref, ho_ref):
    xn = x_ref[...] + _rms(t_ref[...], gp_ref[...])
    xo_ref[...] = xn
    ho_ref[...] = _rms(xn, gn_ref[...]).astype(ho_ref.dtype)


def _resnorm_last_kernel(x_ref, t_ref, gp_ref, xo_ref):
    xo_ref[...] = x_ref[...] + _rms(t_ref[...], gp_ref[...])


def residual_norm(x, t, g_post, g_next=None, *, bm=256):
    s, d = x.shape
    row = pl.BlockSpec((bm, d), lambda i: (i, 0))
    vec = pl.BlockSpec((1, d), lambda i: (0, 0))
    if g_next is None:
        return pl.pallas_call(
            _resnorm_last_kernel, grid=(s // bm,),
            in_specs=[row, row, vec], out_specs=row,
            out_shape=jax.ShapeDtypeStruct((s, d), F32),
            compiler_params=_params(1), name="residual_norm_last",
        )(x, t, g_post.reshape(1, d))
    return pl.pallas_call(
        _resnorm_kernel, grid=(s // bm,),
        in_specs=[row, row, vec, vec], out_specs=[row, row],
        out_shape=[jax.ShapeDtypeStruct((s, d), F32), jax.ShapeDtypeStruct((s, d), BF16)],
        compiler_params=_params(1), name="residual_norm",
    )(x, t, g_post.reshape(1, d), g_next.reshape(1, d))


def _mm_kernel(x_ref, w_ref, o_ref):
    o_ref[...] = jnp.dot(x_ref[...], w_ref[...],
                         preferred_element_type=F32).astype(o_ref.dtype)


def matmul(x, w, *, n_out, col_off, bm, bn, out_dtype, name):
    m, k = x.shape
    off = col_off // bn
    assert col_off % bn == 0 and n_out % bn == 0 and m % bm == 0
    return pl.pallas_call(
        _mm_kernel,
        grid=(m // bm, n_out // bn),
        in_specs=[pl.BlockSpec((bm, k), lambda i, j: (i, 0)),
                  pl.BlockSpec((k, bn), lambda i, j: (0, j + off))],
        out_specs=pl.BlockSpec((bm, bn), lambda i, j: (i, j)),
        out_shape=jax.ShapeDtypeStruct((m, n_out), out_dtype),
        compiler_params=_params(2),
        name=name,
    )(x, w)


def _attn_kernel(sink_ref, q_ref, kp_ref, kc_ref, vp_ref, vc_ref, bias_ref, o_ref):
    hp = pl.program_id(1)
    nkeys = 2 * ATTN_BLOCK
    bias = bias_ref[...]
    lo = lax.broadcasted_iota(jnp.int32, (nkeys, LANES), 1) < HEAD_DIM
    k2 = jnp.concatenate([kp_ref[...], kc_ref[...]], axis=0).astype(F32)
    v2 = jnp.concatenate([vp_ref[...], vc_ref[...]], axis=0).astype(F32)
    k2r = pltpu.roll(k2, HEAD_DIM, axis=1)
    v2r = pltpu.roll(v2, HEAD_DIM, axis=1)
    zero = jnp.zeros_like(k2)
    rows_per_kv = GQA_GROUP // 2
    for hh in range(2):
        kd = jnp.where(lo, k2, k2r) if hh == 0 else jnp.where(lo, k2r, k2)
        vd = jnp.where(lo, v2, v2r) if hh == 0 else jnp.where(lo, v2r, v2)
        kbd = jnp.concatenate([jnp.where(lo, kd, zero), jnp.where(lo, zero, kd)],
                              axis=0).astype(BF16)
        vbd = jnp.concatenate([jnp.where(lo, vd, zero), jnp.where(lo, zero, vd)],
                              axis=0).astype(BF16)
        base = hh * rows_per_kv * LANES
        qs = jnp.concatenate(
            [q_ref[:, base + r * LANES: base + (r + 1) * LANES] for r in range(rows_per_kv)],
            axis=0)
        s = lax.dot_general(qs, kbd, (((1,), (1,)), ((), ())),
                            preferred_element_type=F32)
        s = s * (HEAD_DIM ** -0.5) + bias
        p_rows = []
        for r in range(rows_per_kv):
            halves = []
            for c in range(2):
                sink = sink_ref[(2 * hp + hh) * GQA_GROUP + 2 * r + c]
                blk = s[r * ATTN_BLOCK:(r + 1) * ATTN_BLOCK, c * nkeys:(c + 1) * nkeys]
                m = jnp.maximum(jnp.max(blk, axis=-1, keepdims=True), sink)
                e = jnp.exp(blk - m)
                denom = jnp.sum(e, axis=-1, keepdims=True) + jnp.exp(sink - m)
                halves.append((e * (1.0 / denom)).astype(BF16))
            p_rows.append(jnp.concatenate(halves, axis=1))
        p = jnp.concatenate(p_rows, axis=0)
        o = jnp.dot(p, vbd, preferred_element_type=F32)
        for r in range(rows_per_kv):
            o_ref[:, base + r * LANES: base + (r + 1) * LANES] = (
                o[r * ATTN_BLOCK:(r + 1) * ATTN_BLOCK].astype(o_ref.dtype))


def _attn_bias():
    nkeys = 2 * ATTN_BLOCK
    rows = (GQA_GROUP // 2) * ATTN_BLOCK
    qi = (jnp.arange(rows) % ATTN_BLOCK)[:, None]
    kj = (jnp.arange(2 * nkeys) % nkeys)[None, :]
    rel = kj - ATTN_BLOCK - qi
    in_band = (rel <= 0) & (rel > -ATTN_BLOCK)
    first = in_band & (kj >= ATTN_BLOCK)
    return jnp.where(jnp.stack([first, in_band]), 0.0, MASK_VALUE).astype(F32)


def sliding_window_attention(qkv, sinks):
    s = qkv.shape[0]
    q_width = N_KV_HEADS * GQA_GROUP * HEAD_DIM
    pair = 2 * GQA_GROUP * HEAD_DIM
    k_blk = q_width // LANES
    v_blk = k_blk + N_KV_HEADS * HEAD_DIM // LANES
    rows = (GQA_GROUP // 2) * ATTN_BLOCK
    prev = lambda n: jnp.maximum(n - 1, 0)
    kv = (ATTN_BLOCK, LANES)
    return pl.pallas_call(
        _attn_kernel,
        grid=(s // ATTN_BLOCK, N_KV_HEADS // 2),
        in_specs=[
            pl.BlockSpec(memory_space=pltpu.SMEM),
            pl.BlockSpec((ATTN_BLOCK, pair), lambda n, hp: (n, hp)),
            pl.BlockSpec(kv, lambda n, hp: (prev(n), k_blk + hp)),
            pl.BlockSpec(kv, lambda n, hp: (n, k_blk + hp)),
            pl.BlockSpec(kv, lambda n, hp: (prev(n), v_blk + hp)),
            pl.BlockSpec(kv, lambda n, hp: (n, v_blk + hp)),
            pl.BlockSpec((None, rows, 4 * ATTN_BLOCK), lambda n, hp: (jnp.minimum(n, 1), 0, 0)),
        ],
        out_specs=pl.BlockSpec((ATTN_BLOCK, pair), lambda n, hp: (n, hp)),
        out_shape=jax.ShapeDtypeStruct((s, q_width), BF16),
        compiler_params=_params(2),
        name="swa_attention",
    )(sinks, qkv, qkv, qkv, qkv, qkv, _attn_bias())


def _pool_kernel(u_ref, halo_ref, wp_ref, sc_ref, o_ref, *, bm, gdim):
    i = pl.program_id(0)
    t = lax.broadcasted_iota(jnp.int32, (bm, 1), 0) + i * bm
    for g, w in enumerate(POOL_WINDOWS):
        cols = slice(g * gdim, (g + 1) * gdim)
        u = u_ref[:, cols]
        halo = jnp.where(i > 0, halo_ref[:, cols], 0.0)
        acc = jnp.concatenate([halo, u], axis=0)
        k = 1
        while k < w:
            acc = acc + pltpu.roll(acc, k, axis=0)
            k *= 2
        count = jnp.minimum(t + 1, w).astype(F32)
        mixed = (acc[POOL_HALO:] * (1.0 / count) - u).astype(BF16)
        y = jnp.dot(mixed, wp_ref[g], preferred_element_type=F32) * sc_ref[:, cols]
        o_ref[:, cols] = y.astype(o_ref.dtype)


def multiscale_pool(u, w_pool, pool_scale, *, bm=256):
    s, width = u.shape
    ngroups, gdim, _ = w_pool.shape
    return pl.pallas_call(
        functools.partial(_pool_kernel, bm=bm, gdim=gdim),
        grid=(s // bm,),
        in_specs=[
            pl.BlockSpec((bm, width), lambda i: (i, 0)),
            pl.BlockSpec((POOL_HALO, width),
                         lambda i: (jnp.maximum(i * (bm // POOL_HALO) - 1, 0), 0)),
            pl.BlockSpec((ngroups, gdim, gdim), lambda i: (0, 0, 0)),
            pl.BlockSpec((1, width), lambda i: (0, 0)),
        ],
        out_specs=pl.BlockSpec((bm, width), lambda i: (i, 0)),
        out_shape=jax.ShapeDtypeStruct((s, width), BF16),
        compiler_params=_params(1),
        name="multiscale_pool",
    )(u, u, w_pool, pool_scale.reshape(1, width))


def _merge_kernel(h_ref, a_ref, p_ref, wga_ref, wgp_ref, wba_ref, wbp_ref, o_ref):
    h = h_ref[...]
    dot = functools.partial(jnp.dot, preferred_element_type=F32)
    ya = jax.nn.sigmoid(dot(h, wga_ref[...])) * dot(a_ref[...], wba_ref[...])
    yp = jax.nn.sigmoid(dot(h, wgp_ref[...])) * dot(p_ref[...], wbp_ref[...])
    o_ref[...] = (ya + yp).astype(o_ref.dtype)


def gated_merge(h, attn, pool, w_in, w_ba, w_bp, *, gate_attn_off, gate_pool_off, bm, bn):
    s, d = h.shape
    ka, kp = attn.shape[1], pool.shape[1]
    ga, gp = gate_attn_off // bn, gate_pool_off // bn
    assert gate_attn_off % bn == 0 and gate_pool_off % bn == 0
    return pl.pallas_call(
        _merge_kernel,
        grid=(s // bm, d // bn),
        in_specs=[
            pl.BlockSpec((bm, d), lambda i, j: (i, 0)),
            pl.BlockSpec((bm, ka), lambda i, j: (i, 0)),
            pl.BlockSpec((bm, kp), lambda i, j: (i, 0)),
            pl.BlockSpec((d, bn), lambda i, j: (0, j + ga)),
            pl.BlockSpec((d, bn), lambda i, j: (0, j + gp)),
            pl.BlockSpec((ka, bn), lambda i, j: (0, j)),
            pl.BlockSpec((kp, bn), lambda i, j: (0, j)),
        ],
        out_specs=pl.BlockSpec((bm, bn), lambda i, j: (i, j)),
        out_shape=jax.ShapeDtypeStruct((s, d), BF16),
        compiler_params=_params(2),
        name="gated_merge",
    )(h, attn, pool, w_in, w_in, w_ba, w_bp)


def _ffn_up_kernel(h_ref, wg_ref, wv_ref, cwg_ref, cwv_ref, cbg_ref, cbv_ref, o_ref,
                   ug_ref, uv_ref, *, bm):
    i = pl.program_id(1)
    h = h_ref[...]

    def conv(u_ref, w_ref, cw_ref, cb_ref):
        @pl.when(i == 0)
        def _():
            u_ref[0:SUBLANES, :] = jnp.zeros((SUBLANES, u_ref.shape[1]), F32)

        @pl.when(i > 0)
        def _():
            u_ref[0:SUBLANES, :] = u_ref[bm:bm + SUBLANES, :]

        u_ref[SUBLANES:bm + SUBLANES, :] = jnp.dot(h, w_ref[...], preferred_element_type=F32)
        out = cb_ref[...]
        for tap in range(CONV_WIDTH):
            start = SUBLANES - (CONV_WIDTH - 1 - tap)
            out = out + u_ref[start:start + bm, :] * cw_ref[tap:tap + 1, :]
        return out

    gate = conv(ug_ref, wg_ref, cwg_ref, cbg_ref)
    val = conv(uv_ref, wv_ref, cwv_ref, cbv_ref)
    o_ref[...] = (jax.nn.gelu(gate, approximate=True) * val).astype(o_ref.dtype)


def conv_geglu_up(h, w_up, conv_w, conv_b, *, bm, bn):
    s, d = h.shape
    d_ff = w_up.shape[1] // 2
    nj = d_ff // bn
    assert d_ff % bn == 0
    return pl.pallas_call(
        functools.partial(_ffn_up_kernel, bm=bm),
        grid=(nj, s // bm),
        in_specs=[
            pl.BlockSpec((bm, d), lambda j, i: (i, 0)),
            pl.BlockSpec((d, bn), lambda j, i: (0, j)),
            pl.BlockSpec((d, bn), lambda j, i: (0, j + nj)),
            pl.BlockSpec((CONV_WIDTH, bn), lambda j, i: (0, j)),
            pl.BlockSpec((CONV_WIDTH, bn), lambda j, i: (0, j + nj)),
            pl.BlockSpec((1, bn), lambda j, i: (0, j)),
            pl.BlockSpec((1, bn), lambda j, i: (0, j + nj)),
        ],
        out_specs=pl.BlockSpec((bm, bn), lambda j, i: (i, j)),
        out_shape=jax.ShapeDtypeStruct((s, d_ff), BF16),
        scratch_shapes=[pltpu.VMEM((bm + SUBLANES, bn), F32),
                        pltpu.VMEM((bm + SUBLANES, bn), F32)],
        compiler_params=_params(2),
        name="conv_geglu_up",
    )(h, w_up, w_up, conv_w, conv_w, conv_b.reshape(1, -1), conv_b.reshape(1, -1))


def _ple_kernel(h_ref, p_ref, wg_ref, wp_ref, o_ref):
    g = jnp.dot(h_ref[...], wg_ref[...], preferred_element_type=F32)
    e = jnp.dot(p_ref[...].astype(BF16), wp_ref[...], preferred_element_type=F32)
    o_ref[...] = jax.nn.sigmoid(g) * e


def ple_gate(h, p, w_gate, w_ple, *, bm, bn):
    s, d = h.shape
    pd = p.shape[1]
    return pl.pallas_call(
        _ple_kernel,
        grid=(s // bm, d // bn),
        in_specs=[
            pl.BlockSpec((bm, d), lambda i, j: (i, 0)),
            pl.BlockSpec((bm, pd), lambda i, j: (i, 0)),
            pl.BlockSpec((d, bn), lambda i, j: (0, j)),
            pl.BlockSpec((pd, bn), lambda i, j: (0, j)),
        ],
        out_specs=pl.BlockSpec((bm, bn), lambda i, j: (i, j)),
        out_shape=jax.ShapeDtypeStruct((s, d), F32),
        compiler_params=_params(2),
        name="ple_gate",
    )(h, p, w_gate, w_ple)


def _layer(x, p, norm_mix_pre, w_in, attn_sinks, w_pool, pool_scale, w_branch_attn,
           w_branch_pool, w_out, norm_mix_post, norm_ffn_pre, w_up, conv_w, conv_b,
           w_down, norm_ffn_post, norm_ple_gate, w_ple_gate, w_ple, norm_ple_post,
           next_pre_gain):
    d = x.shape[1]
    attn_width = w_branch_attn.shape[0]
    pool_width = w_branch_pool.shape[0]
    qkv_width = attn_width + 2 * N_KV_HEADS * HEAD_DIM
    gate_attn_off = qkv_width + pool_width
    gate_pool_off = gate_attn_off + d

    w_in = w_in.astype(BF16)
    h = rmsnorm_cast(x, norm_mix_pre)
    qkv = matmul(h, w_in, n_out=qkv_width, col_off=0, bm=1024, bn=512,
                 out_dtype=BF16, name="proj_qkv")
    u = matmul(h, w_in, n_out=pool_width, col_off=qkv_width, bm=1024, bn=512,
               out_dtype=F32, name="proj_pool_in")
    attn = sliding_window_attention(qkv, attn_sinks)
    pool = multiscale_pool(u, w_pool.astype(BF16), pool_scale)
    merged = gated_merge(h, attn, pool, w_in, w_branch_attn.astype(BF16),
                         w_branch_pool.astype(BF16), gate_attn_off=gate_attn_off,
                         gate_pool_off=gate_pool_off, bm=1024, bn=256)
    t = matmul(merged, w_out.astype(BF16), n_out=d, col_off=0, bm=1024, bn=512,
               out_dtype=F32, name="proj_out")
    x, h = residual_norm(x, t, norm_mix_post, norm_ffn_pre)

    act = conv_geglu_up(h, w_up.astype(BF16), conv_w, conv_b, bm=1024, bn=256)
    t = matmul(act, w_down.astype(BF16), n_out=d, col_off=0, bm=512, bn=512,
               out_dtype=F32, name="proj_down")
    x, h = residual_norm(x, t, norm_ffn_post, norm_ple_gate)

    t = ple_gate(h, p, w_ple_gate.astype(BF16), w_ple.astype(BF16), bm=1024, bn=512)
    return residual_norm(x, t, norm_ple_post, next_pre_gain)


def kernel(x, p, norm_mix_pre, w_in, attn_sinks, w_pool, pool_scale, w_branch_attn,
           w_branch_pool, w_out, norm_mix_post, norm_ffn_pre, w_up, conv_w, conv_b,
           w_down, norm_ffn_post, norm_ple_gate, w_ple_gate, w_ple, norm_ple_post):
    batch, seq, d = x.shape
    depth = w_in.shape[0]
    assert batch == 1
    xs = x.reshape(seq, d)
    for i in range(depth):
        xs = _layer(xs, p[i, 0], norm_mix_pre[i], w_in[i], attn_sinks[i], w_pool[i],
                    pool_scale[i], w_branch_attn[i], w_branch_pool[i], w_out[i],
                    norm_mix_post[i], norm_ffn_pre[i], w_up[i], conv_w[i], conv_b[i],
                    w_down[i], norm_ffn_post[i], norm_ple_gate[i], w_ple_gate[i],
                    w_ple[i], norm_ple_post[i], None)
    return xs.reshape(batch, seq, d)
```

```python
import functools

import jax
import jax.numpy as jnp
from jax import lax
from jax.experimental import pallas as pl
from jax.experimental.pallas import tpu as pltpu

F32 = jnp.float32
BF16 = jnp.bfloat16

HEAD_DIM = 64
GQA_GROUP = 8
N_KV_HEADS = 4
ATTN_BLOCK = 128
POOL_WINDOWS = (2, 4, 8, 16)
POOL_HALO = 16
CONV_WIDTH = 3
RMS_EPS = 1e-6
MASK_VALUE = -1e30

LANES = 128
SUBLANES = 8
VMEM_LIMIT_BYTES = 60 * 1024 * 1024
CAST_ROWS = 512


def _params(n_grid_dims):
    return pltpu.CompilerParams(
        dimension_semantics=("arbitrary",) * n_grid_dims,
        vmem_limit_bytes=VMEM_LIMIT_BYTES)


def _rms(x, gain):
    ms = jnp.mean(x * x, axis=-1, keepdims=True)
    return x * lax.rsqrt(ms + RMS_EPS) * gain


def _norm_kernel(x_ref, g_ref, o_ref):
    o_ref[...] = _rms(x_ref[...], g_ref[...]).astype(o_ref.dtype)


def rmsnorm_cast(x, gain, *, bm=256):
    s, d = x.shape
    return pl.pallas_call(
        _norm_kernel,
        grid=(s // bm,),
        in_specs=[pl.BlockSpec((bm, d), lambda i: (i, 0)),
                  pl.BlockSpec((1, d), lambda i: (0, 0))],
        out_specs=pl.BlockSpec((bm, d), lambda i: (i, 0)),
        out_shape=jax.ShapeDtypeStruct((s, d), BF16),
        compiler_params=_params(1),
        name="rmsnorm_cast",
    )(x, gain.reshape(1, d))


def _resnorm_kernel(x_ref, t_ref, gp_ref, gn_ref, xo_ref, ho_ref):
    xn = x_ref[...] + _rms(t_ref[...], gp_ref[...])
    xo_ref[...] = xn
    ho_ref[...] = _rms(xn, gn_ref[...]).astype(ho_ref.dtype)


def _resnorm_last_kernel(x_ref, t_ref, gp_ref, xo_ref):
    xo_ref[...] = x_ref[...] + _rms(t_ref[...], gp_ref[...])


def residual_norm(x, t, g_post, g_next=None, *, bm=256):
    s, d = x.shape
    row = pl.BlockSpec((bm, d), lambda i: (i, 0))
    vec = pl.BlockSpec((1, d), lambda i: (0, 0))
    if g_next is None:
        return pl.pallas_call(
            _resnorm_last_kernel, grid=(s // bm,),
            in_specs=[row, row, vec], out_specs=row,
            out_shape=jax.ShapeDtypeStruct((s, d), F32),
            compiler_params=_params(1), name="residual_norm_last",
        )(x, t, g_post.reshape(1, d))
    return pl.pallas_call(
        _resnorm_kernel, grid=(s // bm,),
        in_specs=[row, row, vec, vec], out_specs=[row, row],
        out_shape=[jax.ShapeDtypeStruct((s, d), F32), jax.ShapeDtypeStruct((s, d), BF16)],
        compiler_params=_params(1), name="residual_norm",
    )(x, t, g_post.reshape(1, d), g_next.reshape(1, d))


def _mm_kernel(x_ref, w_ref, o_ref):
    o_ref[...] = jnp.dot(x_ref[...], w_ref[...],
                         preferred_element_type=F32).astype(o_ref.dtype)


def matmul(x, w, *, n_out, col_off, bm, bn, out_dtype, name):
    m, k = x.shape
    off = col_off // bn
    assert col_off % bn == 0 and n_out % bn == 0 and m % bm == 0
    return pl.pallas_call(
        _mm_kernel,
        grid=(m // bm, n_out // bn),
        in_specs=[pl.BlockSpec((bm, k), lambda i, j: (i, 0)),
                  pl.BlockSpec((k, bn), lambda i, j: (0, j + off))],
        out_specs=pl.BlockSpec((bm, bn), lambda i, j: (i, j)),
        out_shape=jax.ShapeDtypeStruct((m, n_out), out_dtype),
        compiler_params=_params(2),
        name=name,
    )(x, w)


def _attn_kernel(sink_ref, q_ref, kp_ref, kc_ref, vp_ref, vc_ref, bias_ref, o_ref):
    hp = pl.program_id(1)
    nkeys = 2 * ATTN_BLOCK
    bias = bias_ref[...]
    lo = lax.broadcasted_iota(jnp.int32, (nkeys, LANES), 1) < HEAD_DIM
    k2 = jnp.concatenate([kp_ref[...], kc_ref[...]], axis=0).astype(F32)
    v2 = jnp.concatenate([vp_ref[...], vc_ref[...]], axis=0).astype(F32)
    k2r = pltpu.roll(k2, HEAD_DIM, axis=1)
    v2r = pltpu.roll(v2, HEAD_DIM, axis=1)
    zero = jnp.zeros_like(k2)
    rows_per_kv = GQA_GROUP // 2
    for hh in range(2):
        kd = jnp.where(lo, k2, k2r) if hh == 0 else jnp.where(lo, k2r, k2)
        vd = jnp.where(lo, v2, v2r) if hh == 0 else jnp.where(lo, v2r, v2)
        kbd = jnp.concatenate([jnp.where(lo, kd, zero), jnp.where(lo, zero, kd)],
                              axis=0).astype(BF16)
        vbd = jnp.concatenate([jnp.where(lo, vd, zero), jnp.where(lo, zero, vd)],
                              axis=0).astype(BF16)
        base = hh * rows_per_kv * LANES
        qs = jnp.concatenate(
            [q_ref[:, base + r * LANES: base + (r + 1) * LANES] for r in range(rows_per_kv)],
            axis=0)
        s = lax.dot_general(qs, kbd, (((1,), (1,)), ((), ())),
                            preferred_element_type=F32)
        s = s * (HEAD_DIM ** -0.5) + bias
        p_rows = []
        for r in range(rows_per_kv):
            halves = []
            for c in range(2):
                sink = sink_ref[(2 * hp + hh) * GQA_GROUP + 2 * r + c]
                blk = s[r * ATTN_BLOCK:(r + 1) * ATTN_BLOCK, c * nkeys:(c + 1) * nkeys]
                m = jnp.maximum(jnp.max(blk, axis=-1, keepdims=True), sink)
                e = jnp.exp(blk - m)
                denom = jnp.sum(e, axis=-1, keepdims=True) + jnp.exp(sink - m)
                halves.append((e * (1.0 / denom)).astype(BF16))
            p_rows.append(jnp.concatenate(halves, axis=1))
        p = jnp.concatenate(p_rows, axis=0)
        o = jnp.dot(p, vbd, preferred_element_type=F32)
        for r in range(rows_per_kv):
            o_ref[:, base + r * LANES: base + (r + 1) * LANES] = (
                o[r * ATTN_BLOCK:(r + 1) * ATTN_BLOCK].astype(o_ref.dtype))


def _attn_bias():
    nkeys = 2 * ATTN_BLOCK
    rows = (GQA_GROUP // 2) * ATTN_BLOCK
    qi = (jnp.arange(rows) % ATTN_BLOCK)[:, None]
    kj = (jnp.arange(2 * nkeys) % nkeys)[None, :]
    rel = kj - ATTN_BLOCK - qi
    in_band = (rel <= 0) & (rel > -ATTN_BLOCK)
    first = in_band & (kj >= ATTN_BLOCK)
    return jnp.where(jnp.stack([first, in_band]), 0.0, MASK_VALUE).astype(F32)


def sliding_window_attention(qkv, sinks):
    s = qkv.shape[0]
    q_width = N_KV_HEADS * GQA_GROUP * HEAD_DIM
    pair = 2 * GQA_GROUP * HEAD_DIM
    k_blk = q_width // LANES
    v_blk = k_blk + N_KV_HEADS * HEAD_DIM // LANES
    rows = (GQA_GROUP // 2) * ATTN_BLOCK
    prev = lambda n: jnp.maximum(n - 1, 0)
    kv = (ATTN_BLOCK, LANES)
    return pl.pallas_call(
        _attn_kernel,
        grid=(s // ATTN_BLOCK, N_KV_HEADS // 2),
        in_specs=[
            pl.BlockSpec(memory_space=pltpu.SMEM),
            pl.BlockSpec((ATTN_BLOCK, pair), lambda n, hp: (n, hp)),
            pl.BlockSpec(kv, lambda n, hp: (prev(n), k_blk + hp)),
            pl.BlockSpec(kv, lambda n, hp: (n, k_blk + hp)),
            pl.BlockSpec(kv, lambda n, hp: (prev(n), v_blk + hp)),
            pl.BlockSpec(kv, lambda n, hp: (n, v_blk + hp)),
            pl.BlockSpec((None, rows, 4 * ATTN_BLOCK), lambda n, hp: (jnp.minimum(n, 1), 0, 0)),
        ],
        out_specs=pl.BlockSpec((ATTN_BLOCK, pair), lambda n, hp: (n, hp)),
        out_shape=jax.ShapeDtypeStruct((s, q_width), BF16),
        compiler_params=_params(2),
        name="swa_attention",
    )(sinks, qkv, qkv, qkv, qkv, qkv, _attn_bias())


def _pool_kernel(u_ref, halo_ref, wp_ref, sc_ref, o_ref, *, bm, gdim):
    i = pl.program_id(0)
    t = lax.broadcasted_iota(jnp.int32, (bm, 1), 0) + i * bm
    for g, w in enumerate(POOL_WINDOWS):
        cols = slice(g * gdim, (g + 1) * gdim)
        u = u_ref[:, cols]
        halo = jnp.where(i > 0, halo_ref[:, cols], 0.0)
        acc = jnp.concatenate([halo, u], axis=0)
        k = 1
        while k < w:
            acc = acc + pltpu.roll(acc, k, axis=0)
            k *= 2
        count = jnp.minimum(t + 1, w).astype(F32)
        mixed = (acc[POOL_HALO:] * (1.0 / count) - u).astype(BF16)
        y = jnp.dot(mixed, wp_ref[g], preferred_element_type=F32) * sc_ref[:, cols]
        o_ref[:, cols] = y.astype(o_ref.dtype)


def multiscale_pool(u, w_pool, pool_scale, *, bm=256):
    s, width = u.shape
    ngroups, gdim, _ = w_pool.shape
    return pl.pallas_call(
        functools.partial(_pool_kernel, bm=bm, gdim=gdim),
        grid=(s // bm,),
        in_specs=[
            pl.BlockSpec((bm, width), lambda i: (i, 0)),
            pl.BlockSpec((POOL_HALO, width),
                         lambda i: (jnp.maximum(i * (bm // POOL_HALO) - 1, 0), 0)),
            pl.BlockSpec((ngroups, gdim, gdim), lambda i: (0, 0, 0)),
            pl.BlockSpec((1, width), lambda i: (0, 0)),
        ],
        out_specs=pl.BlockSpec((bm, width), lambda i: (i, 0)),
        out_shape=jax.ShapeDtypeStruct((s, width), BF16),
        compiler_params=_params(1),
        name="multiscale_pool",
    )(u, u, w_pool, pool_scale.reshape(1, width))


def _merge_kernel(h_ref, a_ref, p_ref, wga_ref, wgp_ref, wba_ref, wbp_ref, o_ref):
    h = h_ref[...]
    dot = functools.partial(jnp.dot, preferred_element_type=F32)
    ya = jax.nn.sigmoid(dot(h, wga_ref[...])) * dot(a_ref[...], wba_ref[...])
    yp = jax.nn.sigmoid(dot(h, wgp_ref[...])) * dot(p_ref[...], wbp_ref[...])
    o_ref[...] = (ya + yp).astype(o_ref.dtype)


def gated_merge(h, attn, pool, w_in, w_ba, w_bp, *, gate_attn_off, gate_pool_off, bm, bn):
    s, d = h.shape
    ka, kp = attn.shape[1], pool.shape[1]
    ga, gp = gate_attn_off // bn, gate_pool_off // bn
    assert gate_attn_off % bn == 0 and gate_pool_off % bn == 0
    return pl.pallas_call(
        _merge_kernel,
        grid=(s // bm, d // bn),
        in_specs=[
            pl.BlockSpec((bm, d), lambda i, j: (i, 0)),
            pl.BlockSpec((bm, ka), lambda i, j: (i, 0)),
            pl.BlockSpec((bm, kp), lambda i, j: (i, 0)),
            pl.BlockSpec((d, bn), lambda i, j: (0, j + ga)),
            pl.BlockSpec((d, bn), lambda i, j: (0, j + gp)),
            pl.BlockSpec((ka, bn), lambda i, j: (0, j)),
            pl.BlockSpec((kp, bn), lambda i, j: (0, j)),
        ],
        out_specs=pl.BlockSpec((bm, bn), lambda i, j: (i, j)),
        out_shape=jax.ShapeDtypeStruct((s, d), BF16),
        compiler_params=_params(2),
        name="gated_merge",
    )(h, attn, pool, w_in, w_in, w_ba, w_bp)


def _ffn_up_kernel(h_ref, wg_ref, wv_ref, cwg_ref, cwv_ref, cbg_ref, cbv_ref, o_ref,
                   w_ref, u_ref, *, bm, bn, n_sub):
    i = pl.program_id(1)
    d = h_ref.shape[1]

    @pl.when(i == 0)
    def _():
        def cast_rows(c, carry):
            rows = pl.ds(pl.multiple_of(c * CAST_ROWS, CAST_ROWS), CAST_ROWS)
            w_ref[rows, 0:bn] = wg_ref[rows, :].astype(BF16)
            w_ref[rows, bn:2 * bn] = wv_ref[rows, :].astype(BF16)
            return carry
        lax.fori_loop(0, d // CAST_ROWS, cast_rows, 0)
        u_ref[...] = jnp.zeros(u_ref.shape, F32)

    cw = jnp.concatenate([cwg_ref[...], cwv_ref[...]], axis=1)
    cb = jnp.concatenate([cbg_ref[...], cbv_ref[...]], axis=1)
    sub = bm // n_sub
    prev = u_ref[...]
    ups = [jnp.dot(h_ref[r * sub:(r + 1) * sub, :], w_ref[...], preferred_element_type=F32)
           for r in range(n_sub)]
    for r in range(n_sub):
        lo = r * sub
        up = ups[r]
        ext = jnp.concatenate([prev, up], axis=0)
        y = cb + up * cw[CONV_WIDTH - 1:CONV_WIDTH, :]
        for back in range(1, CONV_WIDTH):
            y = y + (pltpu.roll(ext, back, axis=0)[SUBLANES:]
                     * cw[CONV_WIDTH - 1 - back:CONV_WIDTH - back, :])
        o_ref[lo:lo + sub, :] = (jax.nn.gelu(y[:, :bn], approximate=True)
                                 * y[:, bn:]).astype(o_ref.dtype)
        prev = up[sub - SUBLANES:]
    u_ref[...] = prev


def conv_geglu_up(h, w_up, conv_w, conv_b, *, bm, bn, n_sub):
    s, d = h.shape
    d_ff = w_up.shape[1] // 2
    nj = d_ff // bn
    assert d_ff % bn == 0 and d % CAST_ROWS == 0 and bm % n_sub == 0
    return pl.pallas_call(
        functools.partial(_ffn_up_kernel, bm=bm, bn=bn, n_sub=n_sub),
        grid=(nj, s // bm),
        in_specs=[
            pl.BlockSpec((bm, d), lambda j, i: (i, 0)),
            pl.BlockSpec((d, bn), lambda j, i: (0, j)),
            pl.BlockSpec((d, bn), lambda j, i: (0, j + nj)),
            pl.BlockSpec((CONV_WIDTH, bn), lambda j, i: (0, j)),
            pl.BlockSpec((CONV_WIDTH, bn), lambda j, i: (0, j + nj)),
            pl.BlockSpec((1, bn), lambda j, i: (0, j)),
            pl.BlockSpec((1, bn), lambda j, i: (0, j + nj)),
        ],
        out_specs=pl.BlockSpec((bm, bn), lambda j, i: (i, j)),
        out_shape=jax.ShapeDtypeStruct((s, d_ff), BF16),
        scratch_shapes=[pltpu.VMEM((d, 2 * bn), BF16),
                        pltpu.VMEM((SUBLANES, 2 * bn), F32)],
        compiler_params=_params(2),
        name="conv_geglu_up",
    )(h, w_up, w_up, conv_w, conv_w, conv_b.reshape(1, -1), conv_b.reshape(1, -1))


def _ple_kernel(h_ref, p_ref, wg_ref, wp_ref, o_ref):
    g = jnp.dot(h_ref[...], wg_ref[...], preferred_element_type=F32)
    e = jnp.dot(p_ref[...].astype(BF16), wp_ref[...], preferred_element_type=F32)
    o_ref[...] = jax.nn.sigmoid(g) * e


def ple_gate(h, p, w_gate, w_ple, *, bm, bn):
    s, d = h.shape
    pd = p.shape[1]
    return pl.pallas_call(
        _ple_kernel,
        grid=(s // bm, d // bn),
        in_specs=[
            pl.BlockSpec((bm, d), lambda i, j: (i, 0)),
            pl.BlockSpec((bm, pd), lambda i, j: (i, 0)),
            pl.BlockSpec((d, bn), lambda i, j: (0, j)),
            pl.BlockSpec((pd, bn), lambda i, j: (0, j)),
        ],
        out_specs=pl.BlockSpec((bm, bn), lambda i, j: (i, j)),
        out_shape=jax.ShapeDtypeStruct((s, d), F32),
        compiler_params=_params(2),
        name="ple_gate",
    )(h, p, w_gate, w_ple)


def _layer(x, p, norm_mix_pre, w_in, attn_sinks, w_pool, pool_scale, w_branch_attn,
           w_branch_pool, w_out, norm_mix_post, norm_ffn_pre, w_up, conv_w, conv_b,
           w_down, norm_ffn_post, norm_ple_gate, w_ple_gate, w_ple, norm_ple_post,
           next_pre_gain):
    d = x.shape[1]
    attn_width = w_branch_attn.shape[0]
    pool_width = w_branch_pool.shape[0]
    qkv_width = attn_width + 2 * N_KV_HEADS * HEAD_DIM
    gate_attn_off = qkv_width + pool_width
    gate_pool_off = gate_attn_off + d

    w_in = w_in.astype(BF16)
    h = rmsnorm_cast(x, norm_mix_pre)
    qkv = matmul(h, w_in, n_out=qkv_width, col_off=0, bm=1024, bn=512,
                 out_dtype=BF16, name="proj_qkv")
    u = matmul(h, w_in, n_out=pool_width, col_off=qkv_width, bm=1024, bn=512,
               out_dtype=F32, name="proj_pool_in")
    attn = sliding_window_attention(qkv, attn_sinks)
    pool = multiscale_pool(u, w_pool.astype(BF16), pool_scale)
    merged = gated_merge(h, attn, pool, w_in, w_branch_attn.astype(BF16),
                         w_branch_pool.astype(BF16), gate_attn_off=gate_attn_off,
                         gate_pool_off=gate_pool_off, bm=1024, bn=256)
    t = matmul(merged, w_out.astype(BF16), n_out=d, col_off=0, bm=1024, bn=512,
               out_dtype=F32, name="proj_out")
    x, h = residual_norm(x, t, norm_mix_post, norm_ffn_pre)

    act = conv_geglu_up(h, w_up, conv_w, conv_b, bm=1024, bn=256, n_sub=2)
    t = matmul(act, w_down.astype(BF16), n_out=d, col_off=0, bm=512, bn=512,
               out_dtype=F32, name="proj_down")
    x, h = residual_norm(x, t, norm_ffn_post, norm_ple_gate)

    t = ple_gate(h, p, w_ple_gate.astype(BF16), w_ple.astype(BF16), bm=1024, bn=512)
    return residual_norm(x, t, norm_ple_post, next_pre_gain)


def kernel(x, p, norm_mix_pre, w_in, attn_sinks, w_pool, pool_scale, w_branch_attn,
           w_branch_pool, w_out, norm_mix_post, norm_ffn_pre, w_up, conv_w, conv_b,
           w_down, norm_ffn_post, norm_ple_gate, w_ple_gate, w_ple, norm_ple_post):
    batch, seq, d = x.shape
    depth = w_in.shape[0]
    assert batch == 1
    xs = x.reshape(seq, d)
    for i in range(depth):
        xs = _layer(xs, p[i, 0], norm_mix_pre[i], w_in[i], attn_sinks[i], w_pool[i],
                    pool_scale[i], w_branch_attn[i], w_branch_pool[i], w_out[i],
                    norm_mix_post[i], norm_ffn_pre[i], w_up[i], conv_w[i], conv_b[i],
                    w_down[i], norm_ffn_post[i], norm_ple_gate[i], w_ple_gate[i],
                    w_ple[i], norm_ple_post[i], None)
    return xs.reshape(batch, seq, d)
```

```python
import functools

import jax
import jax.numpy as jnp
from jax import lax
from jax.experimental import pallas as pl
from jax.experimental.pallas import tpu as pltpu

F32 = jnp.float32
BF16 = jnp.bfloat16

HEAD_DIM = 64
GQA_GROUP = 8
N_KV_HEADS = 4
ATTN_BLOCK = 128
POOL_WINDOWS = (2, 4, 8, 16)
POOL_HALO = 16
CONV_WIDTH = 3
RMS_EPS = 1e-6
MASK_VALUE = -1e30

LANES = 128
SUBLANES = 8
VMEM_LIMIT_BYTES = 60 * 1024 * 1024
CAST_ROWS = 512


def _params(n_grid_dims):
    return pltpu.CompilerParams(
        dimension_semantics=("arbitrary",) * n_grid_dims,
        vmem_limit_bytes=VMEM_LIMIT_BYTES)


def _rms(x, gain):
    ms = jnp.mean(x * x, axis=-1, keepdims=True)
    return x * lax.rsqrt(ms + RMS_EPS) * gain


def _norm_kernel(x_ref, g_ref, o_ref):
    o_ref[...] = _rms(x_ref[...], g_ref[...]).astype(o_ref.dtype)


def rmsnorm_cast(x, gain, *, bm=256):
    s, d = x.shape
    return pl.pallas_call(
        _norm_kernel,
        grid=(s // bm,),
        in_specs=[pl.BlockSpec((bm, d), lambda i: (i, 0)),
                  pl.BlockSpec((1, d), lambda i: (0, 0))],
        out_specs=pl.BlockSpec((bm, d), lambda i: (i, 0)),
        out_shape=jax.ShapeDtypeStruct((s, d), BF16),
        compiler_params=_params(1),
        name="rmsnorm_cast",
    )(x, gain.reshape(1, d))


def _resnorm_kernel(x_ref, t_ref, gp_ref, gn_ref, xo_ref, ho_ref):
    xn = x_ref[...] + _rms(t_ref[...], gp_ref[...])
    xo_ref[...] = xn
    ho_ref[...] = _rms(xn, gn_ref[...]).astype(ho_ref.dtype)


def residual_norm(x, t, g_post, g_next, *, bm=256):
    s, d = x.shape
    row = pl.BlockSpec((bm, d), lambda i: (i, 0))
    vec = pl.BlockSpec((1, d), lambda i: (0, 0))
    return pl.pallas_call(
        _resnorm_kernel, grid=(s // bm,),
        in_specs=[row, row, vec, vec], out_specs=[row, row],
        out_shape=[jax.ShapeDtypeStruct((s, d), F32), jax.ShapeDtypeStruct((s, d), BF16)],
        compiler_params=_params(1), name="residual_norm",
    )(x, t, g_post.reshape(1, d), g_next.reshape(1, d))


def _mm_kernel(x_ref, w_ref, o_ref):
    o_ref[...] = jnp.dot(x_ref[...], w_ref[...],
                         preferred_element_type=F32).astype(o_ref.dtype)


def _stage_bf16(dst_ref, col0, src_ref):
    k, n = src_ref.shape

    def cast_rows(c, carry):
        rows = pl.ds(pl.multiple_of(c * CAST_ROWS, CAST_ROWS), CAST_ROWS)
        dst_ref[rows, col0:col0 + n] = src_ref[rows, :].astype(BF16)
        return carry
    lax.fori_loop(0, k // CAST_ROWS, cast_rows, 0)


def _mm_wcast_kernel(x_ref, w_ref, o_ref, wb_ref):
    @pl.when(pl.program_id(1) == 0)
    def _():
        _stage_bf16(wb_ref, 0, w_ref)
    o_ref[...] = jnp.dot(x_ref[...], wb_ref[...],
                         preferred_element_type=F32).astype(o_ref.dtype)


def matmul_wcast(x, w, *, n_out, col_off, bm, bn, out_dtype, name):
    m, k = x.shape
    off = col_off // bn
    assert col_off % bn == 0 and n_out % bn == 0 and m % bm == 0 and k % CAST_ROWS == 0
    return pl.pallas_call(
        _mm_wcast_kernel,
        grid=(n_out // bn, m // bm),
        in_specs=[pl.BlockSpec((bm, k), lambda j, i: (i, 0)),
                  pl.BlockSpec((k, bn), lambda j, i: (0, j + off))],
        out_specs=pl.BlockSpec((bm, bn), lambda j, i: (i, j)),
        out_shape=jax.ShapeDtypeStruct((m, n_out), out_dtype),
        scratch_shapes=[pltpu.VMEM((k, bn), BF16)],
        compiler_params=_params(2),
        name=name,
    )(x, w)


def matmul(x, w, *, n_out, col_off, bm, bn, out_dtype, name):
    m, k = x.shape
    off = col_off // bn
    assert col_off % bn == 0 and n_out % bn == 0 and m % bm == 0
    return pl.pallas_call(
        _mm_kernel,
        grid=(m // bm, n_out // bn),
        in_specs=[pl.BlockSpec((bm, k), lambda i, j: (i, 0)),
                  pl.BlockSpec((k, bn), lambda i, j: (0, j + off))],
        out_specs=pl.BlockSpec((bm, bn), lambda i, j: (i, j)),
        out_shape=jax.ShapeDtypeStruct((m, n_out), out_dtype),
        compiler_params=_params(2),
        name=name,
    )(x, w)


def _attn_kernel(sink_ref, q_ref, kp_ref, kc_ref, vp_ref, vc_ref, bias_ref, o_ref):
    hp = pl.program_id(1)
    nkeys = 2 * ATTN_BLOCK
    bias = bias_ref[...]
    lo = lax.broadcasted_iota(jnp.int32, (nkeys, LANES), 1) < HEAD_DIM
    k2 = jnp.concatenate([kp_ref[...], kc_ref[...]], axis=0).astype(F32)
    v2 = jnp.concatenate([vp_ref[...], vc_ref[...]], axis=0).astype(F32)
    k2r = pltpu.roll(k2, HEAD_DIM, axis=1)
    v2r = pltpu.roll(v2, HEAD_DIM, axis=1)
    zero = jnp.zeros_like(k2)
    lo_q = lax.broadcasted_iota(jnp.int32, (ATTN_BLOCK, LANES), 1) < HEAD_DIM
    rows_per_kv = GQA_GROUP // 2
    for hh in range(2):
        kd = jnp.where(lo, k2, k2r) if hh == 0 else jnp.where(lo, k2r, k2)
        vd = jnp.where(lo, v2, v2r) if hh == 0 else jnp.where(lo, v2r, v2)
        kbd = jnp.concatenate([jnp.where(lo, kd, zero), jnp.where(lo, zero, kd)],
                              axis=0).astype(BF16)
        vbd = jnp.concatenate([jnp.where(lo, vd, zero), jnp.where(lo, zero, vd)],
                              axis=0).astype(BF16)
        base = hh * rows_per_kv * LANES
        qs = jnp.concatenate(
            [q_ref[:, base + r * LANES: base + (r + 1) * LANES] for r in range(rows_per_kv)],
            axis=0)
        s = lax.dot_general(qs * jnp.asarray(HEAD_DIM ** -0.5, BF16), kbd,
                            (((1,), (1,)), ((), ())),
                            preferred_element_type=F32) + bias
        p_rows, inv_rows = [], []
        for r in range(rows_per_kv):
            halves, invs = [], []
            for c in range(2):
                sink = sink_ref[(2 * hp + hh) * GQA_GROUP + 2 * r + c]
                blk = s[r * ATTN_BLOCK:(r + 1) * ATTN_BLOCK, c * nkeys:(c + 1) * nkeys]
                m = jnp.maximum(jnp.max(blk, axis=-1, keepdims=True), sink)
                e = jnp.exp(blk - m)
                denom = jnp.sum(e, axis=-1, keepdims=True) + jnp.exp(sink - m)
                halves.append(e.astype(BF16))
                invs.append(1.0 / denom)
            p_rows.append(jnp.concatenate(halves, axis=1))
            inv_rows.append(jnp.where(lo_q, invs[0], invs[1]))
        p = jnp.concatenate(p_rows, axis=0)
        o = jnp.dot(p, vbd, preferred_element_type=F32)
        for r in range(rows_per_kv):
            o_ref[:, base + r * LANES: base + (r + 1) * LANES] = (
                o[r * ATTN_BLOCK:(r + 1) * ATTN_BLOCK] * inv_rows[r]).astype(o_ref.dtype)


def _attn_bias():
    nkeys = 2 * ATTN_BLOCK
    rows = (GQA_GROUP // 2) * ATTN_BLOCK
    qi = (jnp.arange(rows) % ATTN_BLOCK)[:, None]
    kj = (jnp.arange(2 * nkeys) % nkeys)[None, :]
    rel = kj - ATTN_BLOCK - qi
    in_band = (rel <= 0) & (rel > -ATTN_BLOCK)
    first = in_band & (kj >= ATTN_BLOCK)
    return jnp.where(jnp.stack([first, in_band]), 0.0, MASK_VALUE).astype(F32)


def sliding_window_attention(qkv, sinks):
    s = qkv.shape[0]
    q_width = N_KV_HEADS * GQA_GROUP * HEAD_DIM
    pair = 2 * GQA_GROUP * HEAD_DIM
    k_blk = q_width // LANES
    v_blk = k_blk + N_KV_HEADS * HEAD_DIM // LANES
    rows = (GQA_GROUP // 2) * ATTN_BLOCK
    prev = lambda n: jnp.maximum(n - 1, 0)
    kv = (ATTN_BLOCK, LANES)
    return pl.pallas_call(
        _attn_kernel,
        grid=(s // ATTN_BLOCK, N_KV_HEADS // 2),
        in_specs=[
            pl.BlockSpec(memory_space=pltpu.SMEM),
            pl.BlockSpec((ATTN_BLOCK, pair), lambda n, hp: (n, hp)),
            pl.BlockSpec(kv, lambda n, hp: (prev(n), k_blk + hp)),
            pl.BlockSpec(kv, lambda n, hp: (n, k_blk + hp)),
            pl.BlockSpec(kv, lambda n, hp: (prev(n), v_blk + hp)),
            pl.BlockSpec(kv, lambda n, hp: (n, v_blk + hp)),
            pl.BlockSpec((None, rows, 4 * ATTN_BLOCK), lambda n, hp: (jnp.minimum(n, 1), 0, 0)),
        ],
        out_specs=pl.BlockSpec((ATTN_BLOCK, pair), lambda n, hp: (n, hp)),
        out_shape=jax.ShapeDtypeStruct((s, q_width), BF16),
        compiler_params=_params(2),
        name="swa_attention",
    )(sinks, qkv, qkv, qkv, qkv, qkv, _attn_bias())


def _pool_kernel(u_ref, halo_ref, wp_ref, sc_ref, o_ref, *, bm, gdim):
    i = pl.program_id(0)
    t = lax.broadcasted_iota(jnp.int32, (bm, 1), 0) + i * bm
    for g, w in enumerate(POOL_WINDOWS):
        cols = slice(g * gdim, (g + 1) * gdim)
        u = u_ref[:, cols]
        halo = jnp.where(i > 0, halo_ref[:, cols], 0.0)
        acc = jnp.concatenate([halo, u], axis=0)
        k = 1
        while k < w:
            acc = acc + pltpu.roll(acc, k, axis=0)
            k *= 2
        count = jnp.minimum(t + 1, w).astype(F32)
        mixed = (acc[POOL_HALO:] * (1.0 / count) - u).astype(BF16)
        y = jnp.dot(mixed, wp_ref[g], preferred_element_type=F32) * sc_ref[:, cols]
        o_ref[:, cols] = y.astype(o_ref.dtype)


def multiscale_pool(u, w_pool, pool_scale, *, bm=256):
    s, width = u.shape
    ngroups, gdim, _ = w_pool.shape
    return pl.pallas_call(
        functools.partial(_pool_kernel, bm=bm, gdim=gdim),
        grid=(s // bm,),
        in_specs=[
            pl.BlockSpec((bm, width), lambda i: (i, 0)),
            pl.BlockSpec((POOL_HALO, width),
                         lambda i: (jnp.maximum(i * (bm // POOL_HALO) - 1, 0), 0)),
            pl.BlockSpec((ngroups, gdim, gdim), lambda i: (0, 0, 0)),
            pl.BlockSpec((1, width), lambda i: (0, 0)),
        ],
        out_specs=pl.BlockSpec((bm, width), lambda i: (i, 0)),
        out_shape=jax.ShapeDtypeStruct((s, width), BF16),
        compiler_params=_params(1),
        name="multiscale_pool",
    )(u, u, w_pool, pool_scale.reshape(1, width))


def _merge_kernel(h_ref, a_ref, p_ref, wga_ref, wgp_ref, wba_ref, wbp_ref, o_ref):
    h = h_ref[...]
    dot = functools.partial(jnp.dot, preferred_element_type=F32)
    ya = jax.nn.sigmoid(dot(h, wga_ref[...])) * dot(a_ref[...], wba_ref[...])
    yp = jax.nn.sigmoid(dot(h, wgp_ref[...])) * dot(p_ref[...], wbp_ref[...])
    o_ref[...] = (ya + yp).astype(o_ref.dtype)


def gated_merge(h, attn, pool, w_in, w_ba, w_bp, *, gate_attn_off, gate_pool_off, bm, bn):
    s, d = h.shape
    ka, kp = attn.shape[1], pool.shape[1]
    ga, gp = gate_attn_off // bn, gate_pool_off // bn
    assert gate_attn_off % bn == 0 and gate_pool_off % bn == 0
    return pl.pallas_call(
        _merge_kernel,
        grid=(s // bm, d // bn),
        in_specs=[
            pl.BlockSpec((bm, d), lambda i, j: (i, 0)),
            pl.BlockSpec((bm, ka), lambda i, j: (i, 0)),
            pl.BlockSpec((bm, kp), lambda i, j: (i, 0)),
            pl.BlockSpec((d, bn), lambda i, j: (0, j + ga)),
            pl.BlockSpec((d, bn), lambda i, j: (0, j + gp)),
            pl.BlockSpec((ka, bn), lambda i, j: (0, j)),
            pl.BlockSpec((kp, bn), lambda i, j: (0, j)),
        ],
        out_specs=pl.BlockSpec((bm, bn), lambda i, j: (i, j)),
        out_shape=jax.ShapeDtypeStruct((s, d), BF16),
        compiler_params=_params(2),
        name="gated_merge",
    )(h, attn, pool, w_in, w_in, w_ba, w_bp)


def _ffn_up_kernel(h_ref, wg_ref, wv_ref, cwg_ref, cwv_ref, cbg_ref, cbv_ref, o_ref,
                   w_ref, u_ref, *, bm, bn, n_sub):
    i = pl.program_id(1)

    @pl.when(i == 0)
    def _():
        _stage_bf16(w_ref, 0, wg_ref)
        _stage_bf16(w_ref, bn, wv_ref)
        u_ref[...] = jnp.zeros(u_ref.shape, F32)

    cw = jnp.concatenate([cwg_ref[...], cwv_ref[...]], axis=1)
    cb = jnp.concatenate([cbg_ref[...], cbv_ref[...]], axis=1)
    sub = bm // n_sub
    prev = u_ref[...]
    ups = [jnp.dot(h_ref[r * sub:(r + 1) * sub, :], w_ref[...], preferred_element_type=F32)
           for r in range(n_sub)]
    for r in range(n_sub):
        lo = r * sub
        up = ups[r]
        ext = jnp.concatenate([prev, up], axis=0)
        y = cb + up * cw[CONV_WIDTH - 1:CONV_WIDTH, :]
        for back in range(1, CONV_WIDTH):
            y = y + (pltpu.roll(ext, back, axis=0)[SUBLANES:]
                     * cw[CONV_WIDTH - 1 - back:CONV_WIDTH - back, :])
        o_ref[lo:lo + sub, :] = (jax.nn.gelu(y[:, :bn], approximate=True)
                                 * y[:, bn:]).astype(o_ref.dtype)
        prev = up[sub - SUBLANES:]
    u_ref[...] = prev


def conv_geglu_up(h, w_up, conv_w, conv_b, *, bm, bn, n_sub):
    s, d = h.shape
    d_ff = w_up.shape[1] // 2
    nj = d_ff // bn
    assert d_ff % bn == 0 and d % CAST_ROWS == 0 and bm % n_sub == 0
    return pl.pallas_call(
        functools.partial(_ffn_up_kernel, bm=bm, bn=bn, n_sub=n_sub),
        grid=(nj, s // bm),
        in_specs=[
            pl.BlockSpec((bm, d), lambda j, i: (i, 0)),
            pl.BlockSpec((d, bn), lambda j, i: (0, j)),
            pl.BlockSpec((d, bn), lambda j, i: (0, j + nj)),
            pl.BlockSpec((CONV_WIDTH, bn), lambda j, i: (0, j)),
            pl.BlockSpec((CONV_WIDTH, bn), lambda j, i: (0, j + nj)),
            pl.BlockSpec((1, bn), lambda j, i: (0, j)),
            pl.BlockSpec((1, bn), lambda j, i: (0, j + nj)),
        ],
        out_specs=pl.BlockSpec((bm, bn), lambda j, i: (i, j)),
        out_shape=jax.ShapeDtypeStruct((s, d_ff), BF16),
        scratch_shapes=[pltpu.VMEM((d, 2 * bn), BF16),
                        pltpu.VMEM((SUBLANES, 2 * bn), F32)],
        compiler_params=_params(2),
        name="conv_geglu_up",
    )(h, w_up, w_up, conv_w, conv_w, conv_b.reshape(1, -1), conv_b.reshape(1, -1))


def _stash_tile(t, res_ref, xo_ref, r_ref, ss_ref, *, bn):
    j = pl.program_id(1)
    xo_ref[:, pl.ds(pl.multiple_of(j * bn, bn), bn)] = t
    r_ref[j] = res_ref[...]
    ss_prev = jnp.where(j == 0, 0.0, ss_ref[...])
    ss_ref[...] = ss_prev + jnp.sum(t * t, axis=-1, keepdims=True)


def _finish_rows(r_ref, ss_ref, gp_ref, gn_ref, xo_ref, ho_ref, *, bn):
    nj = r_ref.shape[0]
    inv_d = 1.0 / (nj * bn)
    scale = lax.rsqrt(ss_ref[...] * inv_d + RMS_EPS)
    ss_new = jnp.zeros_like(scale)
    for jj in range(nj):
        cols = slice(jj * bn, (jj + 1) * bn)
        xn = r_ref[jj] + xo_ref[:, cols] * scale * gp_ref[:, cols]
        xo_ref[:, cols] = xn
        ss_new = ss_new + jnp.sum(xn * xn, axis=-1, keepdims=True)
    if ho_ref is not None:
        scale_next = lax.rsqrt(ss_new * inv_d + RMS_EPS)
        for jj in range(nj):
            cols = slice(jj * bn, (jj + 1) * bn)
            ho_ref[:, cols] = (xo_ref[:, cols] * scale_next * gn_ref[:, cols]).astype(ho_ref.dtype)


def _mm_resnorm_kernel(a_ref, w_ref, res_ref, gp_ref, gn_ref, xo_ref, ho_ref,
                       r_ref, ss_ref, *, bn):
    t = jnp.dot(a_ref[...], w_ref[...], preferred_element_type=F32)
    _stash_tile(t, res_ref, xo_ref, r_ref, ss_ref, bn=bn)

    @pl.when(pl.program_id(1) == pl.num_programs(1) - 1)
    def _():
        _finish_rows(r_ref, ss_ref, gp_ref, gn_ref, xo_ref, ho_ref, bn=bn)


def matmul_residual_norm(a, w, resid, g_post, g_next, *, bm, bn):
    s, k = a.shape
    d = w.shape[1]
    nj = d // bn
    assert s % bm == 0 and d % bn == 0
    row = pl.BlockSpec((bm, d), lambda i, j: (i, 0))
    vec = pl.BlockSpec((1, d), lambda i, j: (0, 0))
    return pl.pallas_call(
        functools.partial(_mm_resnorm_kernel, bn=bn),
        grid=(s // bm, nj),
        in_specs=[pl.BlockSpec((bm, k), lambda i, j: (i, 0)),
                  pl.BlockSpec((k, bn), lambda i, j: (0, j)),
                  pl.BlockSpec((bm, bn), lambda i, j: (i, j)),
                  vec, vec],
        out_specs=[row, row],
        out_shape=[jax.ShapeDtypeStruct((s, d), F32), jax.ShapeDtypeStruct((s, d), BF16)],
        scratch_shapes=[pltpu.VMEM((nj, bm, bn), F32), pltpu.VMEM((bm, 1), F32)],
        compiler_params=_params(2),
        name="proj_out_norm",
    )(a, w, resid, g_post.reshape(1, d), g_next.reshape(1, d))


def _ple_resnorm_kernel(h_ref, p_ref, wg_ref, wp_ref, res_ref, gp_ref, xo_ref,
                        r_ref, ss_ref, *, bn):
    g = jnp.dot(h_ref[...], wg_ref[...], preferred_element_type=F32)
    e = jnp.dot(p_ref[...].astype(BF16), wp_ref[...], preferred_element_type=F32)
    _stash_tile(jax.nn.sigmoid(g) * e, res_ref, xo_ref, r_ref, ss_ref, bn=bn)

    @pl.when(pl.program_id(1) == pl.num_programs(1) - 1)
    def _():
        _finish_rows(r_ref, ss_ref, gp_ref, None, xo_ref, None, bn=bn)


def ple_residual_norm(h, p, w_gate, w_ple, resid, g_post, *, bm, bn):
    s, d = h.shape
    pd = p.shape[1]
    nj = d // bn
    assert s % bm == 0 and d % bn == 0
    return pl.pallas_call(
        functools.partial(_ple_resnorm_kernel, bn=bn),
        grid=(s // bm, nj),
        in_specs=[
            pl.BlockSpec((bm, d), lambda i, j: (i, 0)),
            pl.BlockSpec((bm, pd), lambda i, j: (i, 0)),
            pl.BlockSpec((d, bn), lambda i, j: (0, j)),
            pl.BlockSpec((pd, bn), lambda i, j: (0, j)),
            pl.BlockSpec((bm, bn), lambda i, j: (i, j)),
            pl.BlockSpec((1, d), lambda i, j: (0, 0)),
        ],
        out_specs=pl.BlockSpec((bm, d), lambda i, j: (i, 0)),
        out_shape=jax.ShapeDtypeStruct((s, d), F32),
        scratch_shapes=[pltpu.VMEM((nj, bm, bn), F32), pltpu.VMEM((bm, 1), F32)],
        compiler_params=_params(2),
        name="ple_norm",
    )(h, p, w_gate, w_ple, resid, g_post.reshape(1, d))


def _layer(x, p, norm_mix_pre, w_in, attn_sinks, w_pool, pool_scale, w_branch_attn,
           w_branch_pool, w_out, norm_mix_post, norm_ffn_pre, w_up, conv_w, conv_b,
           w_down, norm_ffn_post, norm_ple_gate, w_ple_gate, w_ple, norm_ple_post):
    d = x.shape[1]
    attn_width = w_branch_attn.shape[0]
    pool_width = w_branch_pool.shape[0]
    qkv_width = attn_width + 2 * N_KV_HEADS * HEAD_DIM
    gates_off = qkv_width + pool_width

    h = rmsnorm_cast(x, norm_mix_pre)
    qkv = matmul_wcast(h, w_in, n_out=qkv_width, col_off=0, bm=1024, bn=512,
                       out_dtype=BF16, name="proj_qkv")
    u = matmul_wcast(h, w_in, n_out=pool_width, col_off=qkv_width, bm=1024, bn=512,
                     out_dtype=F32, name="proj_pool_in")
    attn = sliding_window_attention(qkv, attn_sinks)
    pool = multiscale_pool(u, w_pool.astype(BF16), pool_scale)
    w_gates = w_in[:, gates_off:].astype(BF16)
    merged = gated_merge(h, attn, pool, w_gates, w_branch_attn.astype(BF16),
                         w_branch_pool.astype(BF16), gate_attn_off=0,
                         gate_pool_off=d, bm=1024, bn=256)
    x, h = matmul_residual_norm(merged, w_out.astype(BF16), x, norm_mix_post, norm_ffn_pre,
                                bm=512, bn=512)

    act = conv_geglu_up(h, w_up, conv_w, conv_b, bm=1024, bn=256, n_sub=2)
    t = matmul(act, w_down.astype(BF16), n_out=d, col_off=0, bm=512, bn=512,
               out_dtype=F32, name="proj_down")
    x, h = residual_norm(x, t, norm_ffn_post, norm_ple_gate)

    return ple_residual_norm(h, p, w_ple_gate.astype(BF16), w_ple.astype(BF16), x,
                             norm_ple_post, bm=512, bn=512)


def kernel(x, p, norm_mix_pre, w_in, attn_sinks, w_pool, pool_scale, w_branch_attn,
           w_branch_pool, w_out, norm_mix_post, norm_ffn_pre, w_up, conv_w, conv_b,
           w_down, norm_ffn_post, norm_ple_gate, w_ple_gate, w_ple, norm_ple_post):
    batch, seq, d = x.shape
    depth = w_in.shape[0]
    assert batch == 1
    xs = x.reshape(seq, d)
    for i in range(depth):
        xs = _layer(xs, p[i, 0], norm_mix_pre[i], w_in[i], attn_sinks[i], w_pool[i],
                    pool_scale[i], w_branch_attn[i], w_branch_pool[i], w_out[i],
                    norm_mix_post[i], norm_ffn_pre[i], w_up[i], conv_w[i], conv_b[i],
                    w_down[i], norm_ffn_post[i], norm_ple_gate[i], w_ple_gate[i],
                    w_ple[i], norm_ple_post[i])
    return xs.reshape(batch, seq, d)
```

```python
import functools

import jax
import jax.numpy as jnp
from jax import lax
from jax.experimental import pallas as pl
from jax.experimental.pallas import tpu as pltpu

F32 = jnp.float32
BF16 = jnp.bfloat16

HEAD_DIM = 64
GQA_GROUP = 8
N_KV_HEADS = 4
ATTN_BLOCK = 128
POOL_WINDOWS = (2, 4, 8, 16)
POOL_HALO = 16
CONV_WIDTH = 3
RMS_EPS = 1e-6
MASK_VALUE = -1e30

LANES = 128
SUBLANES = 8
VMEM_LIMIT_BYTES = 60 * 1024 * 1024
CAST_ROWS = 512


def _params(n_grid_dims):
    return pltpu.CompilerParams(
        dimension_semantics=("arbitrary",) * n_grid_dims,
        vmem_limit_bytes=VMEM_LIMIT_BYTES)


def _rms(x, gain):
    ms = jnp.mean(x * x, axis=-1, keepdims=True)
    return x * lax.rsqrt(ms + RMS_EPS) * gain


def _norm_kernel(x_ref, g_ref, o_ref):
    o_ref[...] = _rms(x_ref[...], g_ref[...]).astype(o_ref.dtype)


def rmsnorm_cast(x, gain, *, bm=256):
    s, d = x.shape
    return pl.pallas_call(
        _norm_kernel,
        grid=(s // bm,),
        in_specs=[pl.BlockSpec((bm, d), lambda i: (i, 0)),
                  pl.BlockSpec((1, d), lambda i: (0, 0))],
        out_specs=pl.BlockSpec((bm, d), lambda i: (i, 0)),
        out_shape=jax.ShapeDtypeStruct((s, d), BF16),
        compiler_params=_params(1),
        name="rmsnorm_cast",
    )(x, gain.reshape(1, d))


def _resnorm_kernel(x_ref, t_ref, gp_ref, gn_ref, xo_ref, ho_ref):
    xn = x_ref[...] + _rms(t_ref[...], gp_ref[...])
    xo_ref[...] = xn
    ho_ref[...] = _rms(xn, gn_ref[...]).astype(ho_ref.dtype)


def residual_norm(x, t, g_post, g_next, *, bm=256):
    s, d = x.shape
    row = pl.BlockSpec((bm, d), lambda i: (i, 0))
    vec = pl.BlockSpec((1, d), lambda i: (0, 0))
    return pl.pallas_call(
        _resnorm_kernel, grid=(s // bm,),
        in_specs=[row, row, vec, vec], out_specs=[row, row],
        out_shape=[jax.ShapeDtypeStruct((s, d), F32), jax.ShapeDtypeStruct((s, d), BF16)],
        compiler_params=_params(1), name="residual_norm",
    )(x, t, g_post.reshape(1, d), g_next.reshape(1, d))


def _mm_kernel(x_ref, w_ref, o_ref):
    o_ref[...] = jnp.dot(x_ref[...], w_ref[...],
                         preferred_element_type=F32).astype(o_ref.dtype)


def _side_cast_specs(src, n_steps, step_of, n_cols=None):
    k, width = src.shape
    n_cols = width if n_cols is None else n_cols
    rows = k // n_steps
    assert k % n_steps == 0 and rows % (2 * SUBLANES) == 0
    index = lambda *g: (step_of(*g), 0)
    return (pl.BlockSpec((rows, width), index), pl.BlockSpec((rows, n_cols), index),
            jax.ShapeDtypeStruct((k, n_cols), BF16))


def _side_cast(src_ref, dst_ref):
    dst_ref[...] = src_ref[:, src_ref.shape[1] - dst_ref.shape[1]:].astype(BF16)


def _stage_bf16(dst_ref, col0, src_ref):
    k, n = src_ref.shape

    def cast_rows(c, carry):
        rows = pl.ds(pl.multiple_of(c * CAST_ROWS, CAST_ROWS), CAST_ROWS)
        dst_ref[rows, col0:col0 + n] = src_ref[rows, :].astype(BF16)
        return carry
    lax.fori_loop(0, k // CAST_ROWS, cast_rows, 0)


def _mm_wcast_kernel(x_ref, w_ref, *rest, n_side):
    side_src, o_ref = rest[:n_side], rest[n_side]
    side_dst, wb_ref = rest[n_side + 1:2 * n_side + 1], rest[2 * n_side + 1]

    @pl.when(pl.program_id(1) == 0)
    def _():
        _stage_bf16(wb_ref, 0, w_ref)
    o_ref[...] = jnp.dot(x_ref[...], wb_ref[...],
                         preferred_element_type=F32).astype(o_ref.dtype)
    for src_ref, dst_ref in zip(side_src, side_dst):
        _side_cast(src_ref, dst_ref)


def matmul_wcast(x, w, *, n_out, col_off, bm, bn, out_dtype, name, side_weights=()):
    m, k = x.shape
    off = col_off // bn
    assert col_off % bn == 0 and n_out % bn == 0 and m % bm == 0 and k % CAST_ROWS == 0
    ni = m // bm
    side = [_side_cast_specs(sw, (n_out // bn) * ni, lambda j, i: j * ni + i)
            for sw in side_weights]
    outs = pl.pallas_call(
        functools.partial(_mm_wcast_kernel, n_side=len(side)),
        grid=(n_out // bn, ni),
        in_specs=[pl.BlockSpec((bm, k), lambda j, i: (i, 0)),
                  pl.BlockSpec((k, bn), lambda j, i: (0, j + off))] + [sp[0] for sp in side],
        out_specs=[pl.BlockSpec((bm, bn), lambda j, i: (i, j))] + [sp[1] for sp in side],
        out_shape=[jax.ShapeDtypeStruct((m, n_out), out_dtype)] + [sp[2] for sp in side],
        scratch_shapes=[pltpu.VMEM((k, bn), BF16)],
        compiler_params=_params(2),
        name=name,
    )(x, w, *side_weights)
    return outs if side else outs[0]


def matmul(x, w, *, n_out, col_off, bm, bn, out_dtype, name):
    m, k = x.shape
    off = col_off // bn
    assert col_off % bn == 0 and n_out % bn == 0 and m % bm == 0
    return pl.pallas_call(
        _mm_kernel,
        grid=(m // bm, n_out // bn),
        in_specs=[pl.BlockSpec((bm, k), lambda i, j: (i, 0)),
                  pl.BlockSpec((k, bn), lambda i, j: (0, j + off))],
        out_specs=pl.BlockSpec((bm, bn), lambda i, j: (i, j)),
        out_shape=jax.ShapeDtypeStruct((m, n_out), out_dtype),
        compiler_params=_params(2),
        name=name,
    )(x, w)


def _attn_kernel(sink_ref, q_ref, kp_ref, kc_ref, vp_ref, vc_ref, bias_ref, wsrc_ref,
                 o_ref, wdst_ref):
    _side_cast(wsrc_ref, wdst_ref)
    hp = pl.program_id(1)
    nkeys = 2 * ATTN_BLOCK
    bias = bias_ref[...]
    lo = lax.broadcasted_iota(jnp.int32, (nkeys, LANES), 1) < HEAD_DIM
    k2 = jnp.concatenate([kp_ref[...], kc_ref[...]], axis=0).astype(F32)
    v2 = jnp.concatenate([vp_ref[...], vc_ref[...]], axis=0).astype(F32)
    k2r = pltpu.roll(k2, HEAD_DIM, axis=1)
    v2r = pltpu.roll(v2, HEAD_DIM, axis=1)
    zero = jnp.zeros_like(k2)
    lo_q = lax.broadcasted_iota(jnp.int32, (ATTN_BLOCK, LANES), 1) < HEAD_DIM
    rows_per_kv = GQA_GROUP // 2
    for hh in range(2):
        kd = jnp.where(lo, k2, k2r) if hh == 0 else jnp.where(lo, k2r, k2)
        vd = jnp.where(lo, v2, v2r) if hh == 0 else jnp.where(lo, v2r, v2)
        kbd = jnp.concatenate([jnp.where(lo, kd, zero), jnp.where(lo, zero, kd)],
                              axis=0).astype(BF16)
        vbd = jnp.concatenate([jnp.where(lo, vd, zero), jnp.where(lo, zero, vd)],
                              axis=0).astype(BF16)
        base = hh * rows_per_kv * LANES
        qs = jnp.concatenate(
            [q_ref[:, base + r * LANES: base + (r + 1) * LANES] for r in range(rows_per_kv)],
            axis=0)
        s = lax.dot_general(qs * jnp.asarray(HEAD_DIM ** -0.5, BF16), kbd,
                            (((1,), (1,)), ((), ())),
                            preferred_element_type=F32) + bias
        p_rows, inv_rows = [], []
        for r in range(rows_per_kv):
            halves, invs = [], []
            for c in range(2):
                sink = sink_ref[(2 * hp + hh) * GQA_GROUP + 2 * r + c]
                blk = s[r * ATTN_BLOCK:(r + 1) * ATTN_BLOCK, c * nkeys:(c + 1) * nkeys]
                m = jnp.maximum(jnp.max(blk, axis=-1, keepdims=True), sink)
                e = jnp.exp(blk - m)
                denom = jnp.sum(e, axis=-1, keepdims=True) + jnp.exp(sink - m)
                halves.append(e.astype(BF16))
                invs.append(1.0 / denom)
            p_rows.append(jnp.concatenate(halves, axis=1))
            inv_rows.append(jnp.where(lo_q, invs[0], invs[1]))
        p = jnp.concatenate(p_rows, axis=0)
        o = jnp.dot(p, vbd, preferred_element_type=F32)
        for r in range(rows_per_kv):
            o_ref[:, base + r * LANES: base + (r + 1) * LANES] = (
                o[r * ATTN_BLOCK:(r + 1) * ATTN_BLOCK] * inv_rows[r]).astype(o_ref.dtype)


def _attn_bias():
    nkeys = 2 * ATTN_BLOCK
    rows = (GQA_GROUP // 2) * ATTN_BLOCK
    qi = (jnp.arange(rows) % ATTN_BLOCK)[:, None]
    kj = (jnp.arange(2 * nkeys) % nkeys)[None, :]
    rel = kj - ATTN_BLOCK - qi
    in_band = (rel <= 0) & (rel > -ATTN_BLOCK)
    first = in_band & (kj >= ATTN_BLOCK)
    return jnp.where(jnp.stack([first, in_band]), 0.0, MASK_VALUE).astype(F32)


def sliding_window_attention(qkv, sinks, w_side, n_side_cols):
    s = qkv.shape[0]
    n_hp = N_KV_HEADS // 2
    side_in, side_out, side_shape = _side_cast_specs(
        w_side, (s // ATTN_BLOCK) * n_hp, lambda n, hp: n * n_hp + hp, n_side_cols)
    q_width = N_KV_HEADS * GQA_GROUP * HEAD_DIM
    pair = 2 * GQA_GROUP * HEAD_DIM
    k_blk = q_width // LANES
    v_blk = k_blk + N_KV_HEADS * HEAD_DIM // LANES
    rows = (GQA_GROUP // 2) * ATTN_BLOCK
    prev = lambda n: jnp.maximum(n - 1, 0)
    kv = (ATTN_BLOCK, LANES)
    return pl.pallas_call(
        _attn_kernel,
        grid=(s // ATTN_BLOCK, N_KV_HEADS // 2),
        in_specs=[
            pl.BlockSpec(memory_space=pltpu.SMEM),
            pl.BlockSpec((ATTN_BLOCK, pair), lambda n, hp: (n, hp)),
            pl.BlockSpec(kv, lambda n, hp: (prev(n), k_blk + hp)),
            pl.BlockSpec(kv, lambda n, hp: (n, k_blk + hp)),
            pl.BlockSpec(kv, lambda n, hp: (prev(n), v_blk + hp)),
            pl.BlockSpec(kv, lambda n, hp: (n, v_blk + hp)),
            pl.BlockSpec((None, rows, 4 * ATTN_BLOCK), lambda n, hp: (jnp.minimum(n, 1), 0, 0)),
            side_in,
        ],
        out_specs=[pl.BlockSpec((ATTN_BLOCK, pair), lambda n, hp: (n, hp)), side_out],
        out_shape=[jax.ShapeDtypeStruct((s, q_width), BF16), side_shape],
        compiler_params=_params(2),
        name="swa_attention",
    )(sinks, qkv, qkv, qkv, qkv, qkv, _attn_bias(), w_side)


def _pool_kernel(u_ref, halo_ref, wp_ref, sc_ref, o_ref, *, bm, gdim):
    i = pl.program_id(0)
    t = lax.broadcasted_iota(jnp.int32, (bm, 1), 0) + i * bm
    for g, w in enumerate(POOL_WINDOWS):
        cols = slice(g * gdim, (g + 1) * gdim)
        u = u_ref[:, cols]
        halo = jnp.where(i > 0, halo_ref[:, cols], 0.0)
        acc = jnp.concatenate([halo, u], axis=0)
        k = 1
        while k < w:
            acc = acc + pltpu.roll(acc, k, axis=0)
            k *= 2
        count = jnp.minimum(t + 1, w).astype(F32)
        mixed = (acc[POOL_HALO:] * (1.0 / count) - u).astype(BF16)
        y = jnp.dot(mixed, wp_ref[g], preferred_element_type=F32) * sc_ref[:, cols]
        o_ref[:, cols] = y.astype(o_ref.dtype)


def multiscale_pool(u, w_pool, pool_scale, *, bm=256):
    s, width = u.shape
    ngroups, gdim, _ = w_pool.shape
    return pl.pallas_call(
        functools.partial(_pool_kernel, bm=bm, gdim=gdim),
        grid=(s // bm,),
        in_specs=[
            pl.BlockSpec((bm, width), lambda i: (i, 0)),
            pl.BlockSpec((POOL_HALO, width),
                         lambda i: (jnp.maximum(i * (bm // POOL_HALO) - 1, 0), 0)),
            pl.BlockSpec((ngroups, gdim, gdim), lambda i: (0, 0, 0)),
            pl.BlockSpec((1, width), lambda i: (0, 0)),
        ],
        out_specs=pl.BlockSpec((bm, width), lambda i: (i, 0)),
        out_shape=jax.ShapeDtypeStruct((s, width), BF16),
        compiler_params=_params(1),
        name="multiscale_pool",
    )(u, u, w_pool, pool_scale.reshape(1, width))


def _merge_kernel(h_ref, a_ref, p_ref, wga_ref, wgp_ref, wba_ref, wbp_ref, s0_ref, s1_ref,
                  o_ref, d0_ref, d1_ref):
    h = h_ref[...]
    dot = functools.partial(jnp.dot, preferred_element_type=F32)
    ya = jax.nn.sigmoid(dot(h, wga_ref[...])) * dot(a_ref[...], wba_ref[...])
    yp = jax.nn.sigmoid(dot(h, wgp_ref[...])) * dot(p_ref[...], wbp_ref[...])
    o_ref[...] = (ya + yp).astype(o_ref.dtype)
    _side_cast(s0_ref, d0_ref)
    _side_cast(s1_ref, d1_ref)


def gated_merge(h, attn, pool, w_gates, w_ba, w_bp, side_weights, *, bm, bn):
    s, d = h.shape
    ka, kp = attn.shape[1], pool.shape[1]
    nj = d // bn
    assert d % bn == 0 and s % bm == 0
    side = [_side_cast_specs(sw, (s // bm) * nj, lambda i, j: i * nj + j) for sw in side_weights]
    return pl.pallas_call(
        _merge_kernel,
        grid=(s // bm, nj),
        in_specs=[
            pl.BlockSpec((bm, d), lambda i, j: (i, 0)),
            pl.BlockSpec((bm, ka), lambda i, j: (i, 0)),
            pl.BlockSpec((bm, kp), lambda i, j: (i, 0)),
            pl.BlockSpec((d, bn), lambda i, j: (0, j)),
            pl.BlockSpec((d, bn), lambda i, j: (0, j + nj)),
            pl.BlockSpec((ka, bn), lambda i, j: (0, j)),
            pl.BlockSpec((kp, bn), lambda i, j: (0, j)),
            side[0][0], side[1][0],
        ],
        out_specs=[pl.BlockSpec((bm, bn), lambda i, j: (i, j)), side[0][1], side[1][1]],
        out_shape=[jax.ShapeDtypeStruct((s, d), BF16), side[0][2], side[1][2]],
        compiler_params=_params(2),
        name="gated_merge",
    )(h, attn, pool, w_gates, w_gates, w_ba, w_bp, *side_weights)


def _ffn_up_kernel(h_ref, wg_ref, wv_ref, cwg_ref, cwv_ref, cbg_ref, cbv_ref, wsrc_ref,
                   o_ref, wdst_ref, w_ref, u_ref, *, bm, bn, n_sub):
    i = pl.program_id(1)
    _side_cast(wsrc_ref, wdst_ref)

    @pl.when(i == 0)
    def _():
        _stage_bf16(w_ref, 0, wg_ref)
        _stage_bf16(w_ref, bn, wv_ref)
        u_ref[...] = jnp.zeros(u_ref.shape, F32)

    cw = jnp.concatenate([cwg_ref[...], cwv_ref[...]], axis=1)
    cb = jnp.concatenate([cbg_ref[...], cbv_ref[...]], axis=1)
    sub = bm // n_sub
    prev = u_ref[...]
    ups = [jnp.dot(h_ref[r * sub:(r + 1) * sub, :], w_ref[...], preferred_element_type=F32)
           for r in range(n_sub)]
    for r in range(n_sub):
        lo = r * sub
        up = ups[r]
        ext = jnp.concatenate([prev, up], axis=0)
        y = cb + up * cw[CONV_WIDTH - 1:CONV_WIDTH, :]
        for back in range(1, CONV_WIDTH):
            y = y + (pltpu.roll(ext, back, axis=0)[SUBLANES:]
                     * cw[CONV_WIDTH - 1 - back:CONV_WIDTH - back, :])
        o_ref[lo:lo + sub, :] = (jax.nn.gelu(y[:, :bn], approximate=True)
                                 * y[:, bn:]).astype(o_ref.dtype)
        prev = up[sub - SUBLANES:]
    u_ref[...] = prev


def conv_geglu_up(h, w_up, conv_w, conv_b, w_side, *, bm, bn, n_sub):
    s, d = h.shape
    d_ff = w_up.shape[1] // 2
    nj = d_ff // bn
    ni = s // bm
    assert d_ff % bn == 0 and d % CAST_ROWS == 0 and bm % n_sub == 0
    side_in, side_out, side_shape = _side_cast_specs(w_side, nj * ni, lambda j, i: j * ni + i)
    return pl.pallas_call(
        functools.partial(_ffn_up_kernel, bm=bm, bn=bn, n_sub=n_sub),
        grid=(nj, s // bm),
        in_specs=[
            pl.BlockSpec((bm, d), lambda j, i: (i, 0)),
            pl.BlockSpec((d, bn), lambda j, i: (0, j)),
            pl.BlockSpec((d, bn), lambda j, i: (0, j + nj)),
            pl.BlockSpec((CONV_WIDTH, bn), lambda j, i: (0, j)),
            pl.BlockSpec((CONV_WIDTH, bn), lambda j, i: (0, j + nj)),
            pl.BlockSpec((1, bn), lambda j, i: (0, j)),
            pl.BlockSpec((1, bn), lambda j, i: (0, j + nj)),
            side_in,
        ],
        out_specs=[pl.BlockSpec((bm, bn), lambda j, i: (i, j)), side_out],
        out_shape=[jax.ShapeDtypeStruct((s, d_ff), BF16), side_shape],
        scratch_shapes=[pltpu.VMEM((d, 2 * bn), BF16),
                        pltpu.VMEM((SUBLANES, 2 * bn), F32)],
        compiler_params=_params(2),
        name="conv_geglu_up",
    )(h, w_up, w_up, conv_w, conv_w, conv_b.reshape(1, -1), conv_b.reshape(1, -1), w_side)


def _stash_tile(t, res_ref, xo_ref, r_ref, ss_ref, *, bn):
    j = pl.program_id(1)
    xo_ref[:, pl.ds(pl.multiple_of(j * bn, bn), bn)] = t
    r_ref[j] = res_ref[...]
    ss_prev = jnp.where(j == 0, 0.0, ss_ref[...])
    ss_ref[...] = ss_prev + jnp.sum(t * t, axis=-1, keepdims=True)


def _finish_rows(r_ref, ss_ref, gp_ref, gn_ref, xo_ref, ho_ref, *, bn):
    nj = r_ref.shape[0]
    inv_d = 1.0 / (nj * bn)
    scale = lax.rsqrt(ss_ref[...] * inv_d + RMS_EPS)
    ss_new = jnp.zeros_like(scale)
    for jj in range(nj):
        cols = slice(jj * bn, (jj + 1) * bn)
        xn = r_ref[jj] + xo_ref[:, cols] * scale * gp_ref[:, cols]
        xo_ref[:, cols] = xn
        ss_new = ss_new + jnp.sum(xn * xn, axis=-1, keepdims=True)
    if ho_ref is not None:
        scale_next = lax.rsqrt(ss_new * inv_d + RMS_EPS)
        for jj in range(nj):
            cols = slice(jj * bn, (jj + 1) * bn)
            ho_ref[:, cols] = (xo_ref[:, cols] * scale_next * gn_ref[:, cols]).astype(ho_ref.dtype)


def _mm_resnorm_kernel(a_ref, w_ref, res_ref, gp_ref, gn_ref, xo_ref, ho_ref,
                       r_ref, ss_ref, *, bn):
    t = jnp.dot(a_ref[...], w_ref[...], preferred_element_type=F32)
    _stash_tile(t, res_ref, xo_ref, r_ref, ss_ref, bn=bn)

    @pl.when(pl.program_id(1) == pl.num_programs(1) - 1)
    def _():
        _finish_rows(r_ref, ss_ref, gp_ref, gn_ref, xo_ref, ho_ref, bn=bn)


def matmul_residual_norm(a, w, resid, g_post, g_next, *, bm, bn):
    s, k = a.shape
    d = w.shape[1]
    nj = d // bn
    assert s % bm == 0 and d % bn == 0
    row = pl.BlockSpec((bm, d), lambda i, j: (i, 0))
    vec = pl.BlockSpec((1, d), lambda i, j: (0, 0))
    return pl.pallas_call(
        functools.partial(_mm_resnorm_kernel, bn=bn),
        grid=(s // bm, nj),
        in_specs=[pl.BlockSpec((bm, k), lambda i, j: (i, 0)),
                  pl.BlockSpec((k, bn), lambda i, j: (0, j)),
                  pl.BlockSpec((bm, bn), lambda i, j: (i, j)),
                  vec, vec],
        out_specs=[row, row],
        out_shape=[jax.ShapeDtypeStruct((s, d), F32), jax.ShapeDtypeStruct((s, d), BF16)],
        scratch_shapes=[pltpu.VMEM((nj, bm, bn), F32), pltpu.VMEM((bm, 1), F32)],
        compiler_params=_params(2),
        name="proj_out_norm",
    )(a, w, resid, g_post.reshape(1, d), g_next.reshape(1, d))


def _ple_resnorm_kernel(h_ref, p_ref, wg_ref, wp_ref, res_ref, gp_ref, xo_ref,
                        r_ref, ss_ref, *, bn):
    g = jnp.dot(h_ref[...], wg_ref[...], preferred_element_type=F32)
    e = jnp.dot(p_ref[...].astype(BF16), wp_ref[...], preferred_element_type=F32)
    _stash_tile(jax.nn.sigmoid(g) * e, res_ref, xo_ref, r_ref, ss_ref, bn=bn)

    @pl.when(pl.program_id(1) == pl.num_programs(1) - 1)
    def _():
        _finish_rows(r_ref, ss_ref, gp_ref, None, xo_ref, None, bn=bn)


def ple_residual_norm(h, p, w_gate, w_ple, resid, g_post, *, bm, bn):
    s, d = h.shape
    pd = p.shape[1]
    nj = d // bn
    assert s % bm == 0 and d % bn == 0
    return pl.pallas_call(
        functools.partial(_ple_resnorm_kernel, bn=bn),
        grid=(s // bm, nj),
        in_specs=[
            pl.BlockSpec((bm, d), lambda i, j: (i, 0)),
            pl.BlockSpec((bm, pd), lambda i, j: (i, 0)),
            pl.BlockSpec((d, bn), lambda i, j: (0, j)),
            pl.BlockSpec((pd, bn), lambda i, j: (0, j)),
            pl.BlockSpec((bm, bn), lambda i, j: (i, j)),
            pl.BlockSpec((1, d), lambda i, j: (0, 0)),
        ],
        out_specs=pl.BlockSpec((bm, d), lambda i, j: (i, 0)),
        out_shape=jax.ShapeDtypeStruct((s, d), F32),
        scratch_shapes=[pltpu.VMEM((nj, bm, bn), F32), pltpu.VMEM((bm, 1), F32)],
        compiler_params=_params(2),
        name="ple_norm",
    )(h, p, w_gate, w_ple, resid, g_post.reshape(1, d))


def _layer(x, p, norm_mix_pre, w_in, attn_sinks, w_pool, pool_scale, w_branch_attn,
           w_branch_pool, w_out, norm_mix_post, norm_ffn_pre, w_up, conv_w, conv_b,
           w_down, norm_ffn_post, norm_ple_gate, w_ple_gate, w_ple, norm_ple_post):
    d = x.shape[1]
    attn_width = w_branch_attn.shape[0]
    pool_width = w_branch_pool.shape[0]
    qkv_width = attn_width + 2 * N_KV_HEADS * HEAD_DIM
    h = rmsnorm_cast(x, norm_mix_pre)
    qkv = matmul_wcast(h, w_in, n_out=qkv_width, col_off=0, bm=1024, bn=512,
                       out_dtype=BF16, name="proj_qkv")
    u, w_ba, w_bp = matmul_wcast(h, w_in, n_out=pool_width, col_off=qkv_width, bm=1024, bn=512,
                                 out_dtype=F32, name="proj_pool_in",
                                 side_weights=(w_branch_attn, w_branch_pool))
    attn, w_gates = sliding_window_attention(qkv, attn_sinks, w_in, 2 * d)
    pool = multiscale_pool(u, w_pool.astype(BF16), pool_scale)
    merged, w_out_b, w_ple_gate_b = gated_merge(h, attn, pool, w_gates, w_ba, w_bp,
                                                (w_out, w_ple_gate), bm=1024, bn=256)
    x, h = matmul_residual_norm(merged, w_out_b, x, norm_mix_post, norm_ffn_pre,
                                bm=512, bn=512)

    act, w_down_b = conv_geglu_up(h, w_up, conv_w, conv_b, w_down, bm=1024, bn=256, n_sub=2)
    t = matmul(act, w_down_b, n_out=d, col_off=0, bm=512, bn=512,
               out_dtype=F32, name="proj_down")
    x, h = residual_norm(x, t, norm_ffn_post, norm_ple_gate)

    return ple_residual_norm(h, p, w_ple_gate_b, w_ple.astype(BF16), x,
                             norm_ple_post, bm=512, bn=512)


def kernel(x, p, norm_mix_pre, w_in, attn_sinks, w_pool, pool_scale, w_branch_attn,
           w_branch_pool, w_out, norm_mix_post, norm_ffn_pre, w_up, conv_w, conv_b,
           w_down, norm_ffn_post, norm_ple_gate, w_ple_gate, w_ple, norm_ple_post):
    batch, seq, d = x.shape
    depth = w_in.shape[0]
    assert batch == 1
    xs = x.reshape(seq, d)
    for i in range(depth):
        xs = _layer(xs, p[i, 0], norm_mix_pre[i], w_in[i], attn_sinks[i], w_pool[i],
                    pool_scale[i], w_branch_attn[i], w_branch_pool[i], w_out[i],
                    norm_mix_post[i], norm_ffn_pre[i], w_up[i], conv_w[i], conv_b[i],
                    w_down[i], norm_ffn_post[i], norm_ple_gate[i], w_ple_gate[i],
                    w_ple[i], norm_ple_post[i])
    return xs.reshape(batch, seq, d)
```

```python
import functools

import jax
import jax.numpy as jnp
from jax import lax
from jax.experimental import pallas as pl
from jax.experimental.pallas import tpu as pltpu

F32 = jnp.float32
BF16 = jnp.bfloat16

HEAD_DIM = 64
GQA_GROUP = 8
N_KV_HEADS = 4
ATTN_BLOCK = 128
POOL_WINDOWS = (2, 4, 8, 16)
POOL_HALO = 16
CONV_WIDTH = 3
RMS_EPS = 1e-6
MASK_VALUE = -1e30

LANES = 128
SUBLANES = 8
VMEM_LIMIT_BYTES = 60 * 1024 * 1024
CAST_ROWS = 512


def _params(n_grid_dims):
    return pltpu.CompilerParams(
        dimension_semantics=("arbitrary",) * n_grid_dims,
        vmem_limit_bytes=VMEM_LIMIT_BYTES)


def _rms(x, gain):
    ms = jnp.mean(x * x, axis=-1, keepdims=True)
    return x * lax.rsqrt(ms + RMS_EPS) * gain


def _norm_kernel(x_ref, g_ref, o_ref):
    o_ref[...] = _rms(x_ref[...], g_ref[...]).astype(o_ref.dtype)


def rmsnorm_cast(x, gain, *, bm=256):
    s, d = x.shape
    return pl.pallas_call(
        _norm_kernel,
        grid=(s // bm,),
        in_specs=[pl.BlockSpec((bm, d), lambda i: (i, 0)),
                  pl.BlockSpec((1, d), lambda i: (0, 0))],
        out_specs=pl.BlockSpec((bm, d), lambda i: (i, 0)),
        out_shape=jax.ShapeDtypeStruct((s, d), BF16),
        compiler_params=_params(1),
        name="rmsnorm_cast",
    )(x, gain.reshape(1, d))


def _resnorm_kernel(x_ref, t_ref, gp_ref, gn_ref, xo_ref, ho_ref):
    xn = x_ref[...] + _rms(t_ref[...], gp_ref[...])
    xo_ref[...] = xn
    ho_ref[...] = _rms(xn, gn_ref[...]).astype(ho_ref.dtype)


def residual_norm(x, t, g_post, g_next, *, bm=256):
    s, d = x.shape
    row = pl.BlockSpec((bm, d), lambda i: (i, 0))
    vec = pl.BlockSpec((1, d), lambda i: (0, 0))
    return pl.pallas_call(
        _resnorm_kernel, grid=(s // bm,),
        in_specs=[row, row, vec, vec], out_specs=[row, row],
        out_shape=[jax.ShapeDtypeStruct((s, d), F32), jax.ShapeDtypeStruct((s, d), BF16)],
        compiler_params=_params(1), name="residual_norm",
    )(x, t, g_post.reshape(1, d), g_next.reshape(1, d))


def _mm_kernel(x_ref, w_ref, o_ref):
    o_ref[...] = jnp.dot(x_ref[...], w_ref[...],
                         preferred_element_type=F32).astype(o_ref.dtype)


def _side_cast_specs(src, n_steps, step_of, n_cols=None):
    k, width = src.shape
    n_cols = width if n_cols is None else n_cols
    rows = k // n_steps
    assert k % n_steps == 0 and rows % (2 * SUBLANES) == 0
    src_index = lambda *g: (step_of(*g) * rows, width - n_cols)
    dst_index = lambda *g: (step_of(*g), 0)
    return (pl.BlockSpec((pl.Element(rows), pl.Element(n_cols)), src_index),
            pl.BlockSpec((rows, n_cols), dst_index),
            jax.ShapeDtypeStruct((k, n_cols), BF16))


def _side_cast(src_ref, dst_ref):
    dst_ref[...] = src_ref[...].astype(BF16)


def _stage_bf16(dst_ref, col0, src_ref):
    k, n = src_ref.shape

    def cast_rows(c, carry):
        rows = pl.ds(pl.multiple_of(c * CAST_ROWS, CAST_ROWS), CAST_ROWS)
        dst_ref[rows, col0:col0 + n] = src_ref[rows, :].astype(BF16)
        return carry
    lax.fori_loop(0, k // CAST_ROWS, cast_rows, 0)


def _mm_wcast_kernel(x_ref, w_ref, *rest, n_side):
    side_src, o_ref = rest[:n_side], rest[n_side]
    side_dst, wb_ref = rest[n_side + 1:2 * n_side + 1], rest[2 * n_side + 1]

    @pl.when(pl.program_id(1) == 0)
    def _():
        _stage_bf16(wb_ref, 0, w_ref)
    o_ref[...] = jnp.dot(x_ref[...], wb_ref[...],
                         preferred_element_type=F32).astype(o_ref.dtype)
    for src_ref, dst_ref in zip(side_src, side_dst):
        _side_cast(src_ref, dst_ref)


def matmul_wcast(x, w, *, n_out, col_off, bm, bn, out_dtype, name, side_weights=()):
    m, k = x.shape
    off = col_off // bn
    assert col_off % bn == 0 and n_out % bn == 0 and m % bm == 0 and k % CAST_ROWS == 0
    ni = m // bm
    side = [_side_cast_specs(sw, (n_out // bn) * ni, lambda j, i: j * ni + i)
            for sw in side_weights]
    outs = pl.pallas_call(
        functools.partial(_mm_wcast_kernel, n_side=len(side)),
        grid=(n_out // bn, ni),
        in_specs=[pl.BlockSpec((bm, k), lambda j, i: (i, 0)),
                  pl.BlockSpec((k, bn), lambda j, i: (0, j + off))] + [sp[0] for sp in side],
        out_specs=[pl.BlockSpec((bm, bn), lambda j, i: (i, j))] + [sp[1] for sp in side],
        out_shape=[jax.ShapeDtypeStruct((m, n_out), out_dtype)] + [sp[2] for sp in side],
        scratch_shapes=[pltpu.VMEM((k, bn), BF16)],
        compiler_params=_params(2),
        name=name,
    )(x, w, *side_weights)
    return outs if side else outs[0]


def matmul(x, w, *, n_out, col_off, bm, bn, out_dtype, name):
    m, k = x.shape
    off = col_off // bn
    assert col_off % bn == 0 and n_out % bn == 0 and m % bm == 0
    return pl.pallas_call(
        _mm_kernel,
        grid=(m // bm, n_out // bn),
        in_specs=[pl.BlockSpec((bm, k), lambda i, j: (i, 0)),
                  pl.BlockSpec((k, bn), lambda i, j: (0, j + off))],
        out_specs=pl.BlockSpec((bm, bn), lambda i, j: (i, j)),
        out_shape=jax.ShapeDtypeStruct((m, n_out), out_dtype),
        compiler_params=_params(2),
        name=name,
    )(x, w)


def _attn_kernel(sink_ref, q_ref, kp_ref, kc_ref, vp_ref, vc_ref, bias_ref, wsrc_ref,
                 o_ref, wdst_ref):
    _side_cast(wsrc_ref, wdst_ref)
    hp = pl.program_id(1)
    nkeys = 2 * ATTN_BLOCK
    bias = bias_ref[...]
    lo = lax.broadcasted_iota(jnp.int32, (nkeys, LANES), 1) < HEAD_DIM
    k2 = jnp.concatenate([kp_ref[...], kc_ref[...]], axis=0).astype(F32)
    v2 = jnp.concatenate([vp_ref[...], vc_ref[...]], axis=0).astype(F32)
    k2r = pltpu.roll(k2, HEAD_DIM, axis=1)
    v2r = pltpu.roll(v2, HEAD_DIM, axis=1)
    zero = jnp.zeros_like(k2)
    lo_q = lax.broadcasted_iota(jnp.int32, (ATTN_BLOCK, LANES), 1) < HEAD_DIM
    rows_per_kv = GQA_GROUP // 2
    for hh in range(2):
        kd = jnp.where(lo, k2, k2r) if hh == 0 else jnp.where(lo, k2r, k2)
        vd = jnp.where(lo, v2, v2r) if hh == 0 else jnp.where(lo, v2r, v2)
        kbd = jnp.concatenate([jnp.where(lo, kd, zero), jnp.where(lo, zero, kd)],
                              axis=0).astype(BF16)
        vbd = jnp.concatenate([jnp.where(lo, vd, zero), jnp.where(lo, zero, vd)],
                              axis=0).astype(BF16)
        base = hh * rows_per_kv * LANES
        qs = jnp.concatenate(
            [q_ref[:, base + r * LANES: base + (r + 1) * LANES] for r in range(rows_per_kv)],
            axis=0)
        s = lax.dot_general(qs * jnp.asarray(HEAD_DIM ** -0.5, BF16), kbd,
                            (((1,), (1,)), ((), ())),
                            preferred_element_type=F32) + bias
        p_rows, inv_rows = [], []
        for r in range(rows_per_kv):
            halves, invs = [], []
            for c in range(2):
                sink = sink_ref[(2 * hp + hh) * GQA_GROUP + 2 * r + c]
                blk = s[r * ATTN_BLOCK:(r + 1) * ATTN_BLOCK, c * nkeys:(c + 1) * nkeys]
                m = jnp.maximum(jnp.max(blk, axis=-1, keepdims=True), sink)
                e = jnp.exp(blk - m)
                denom = jnp.sum(e, axis=-1, keepdims=True) + jnp.exp(sink - m)
                halves.append(e.astype(BF16))
                invs.append(1.0 / denom)
            p_rows.append(jnp.concatenate(halves, axis=1))
            inv_rows.append(jnp.where(lo_q, invs[0], invs[1]))
        p = jnp.concatenate(p_rows, axis=0)
        o = jnp.dot(p, vbd, preferred_element_type=F32)
        for r in range(rows_per_kv):
            o_ref[:, base + r * LANES: base + (r + 1) * LANES] = (
                o[r * ATTN_BLOCK:(r + 1) * ATTN_BLOCK] * inv_rows[r]).astype(o_ref.dtype)


def _attn_bias():
    nkeys = 2 * ATTN_BLOCK
    rows = (GQA_GROUP // 2) * ATTN_BLOCK
    qi = (jnp.arange(rows) % ATTN_BLOCK)[:, None]
    kj = (jnp.arange(2 * nkeys) % nkeys)[None, :]
    rel = kj - ATTN_BLOCK - qi
    in_band = (rel <= 0) & (rel > -ATTN_BLOCK)
    first = in_band & (kj >= ATTN_BLOCK)
    return jnp.where(jnp.stack([first, in_band]), 0.0, MASK_VALUE).astype(F32)


def sliding_window_attention(qkv, sinks, w_side, n_side_cols):
    s = qkv.shape[0]
    n_hp = N_KV_HEADS // 2
    side_in, side_out, side_shape = _side_cast_specs(
        w_side, (s // ATTN_BLOCK) * n_hp, lambda n, hp: n * n_hp + hp, n_side_cols)
    q_width = N_KV_HEADS * GQA_GROUP * HEAD_DIM
    pair = 2 * GQA_GROUP * HEAD_DIM
    k_blk = q_width // LANES
    v_blk = k_blk + N_KV_HEADS * HEAD_DIM // LANES
    rows = (GQA_GROUP // 2) * ATTN_BLOCK
    prev = lambda n: jnp.maximum(n - 1, 0)
    kv = (ATTN_BLOCK, LANES)
    return pl.pallas_call(
        _attn_kernel,
        grid=(s // ATTN_BLOCK, N_KV_HEADS // 2),
        in_specs=[
            pl.BlockSpec(memory_space=pltpu.SMEM),
            pl.BlockSpec((ATTN_BLOCK, pair), lambda n, hp: (n, hp)),
            pl.BlockSpec(kv, lambda n, hp: (prev(n), k_blk + hp)),
            pl.BlockSpec(kv, lambda n, hp: (n, k_blk + hp)),
            pl.BlockSpec(kv, lambda n, hp: (prev(n), v_blk + hp)),
            pl.BlockSpec(kv, lambda n, hp: (n, v_blk + hp)),
            pl.BlockSpec((None, rows, 4 * ATTN_BLOCK), lambda n, hp: (jnp.minimum(n, 1), 0, 0)),
            side_in,
        ],
        out_specs=[pl.BlockSpec((ATTN_BLOCK, pair), lambda n, hp: (n, hp)), side_out],
        out_shape=[jax.ShapeDtypeStruct((s, q_width), BF16), side_shape],
        compiler_params=_params(2),
        name="swa_attention",
    )(sinks, qkv, qkv, qkv, qkv, qkv, _attn_bias(), w_side)


def _pool_kernel(u_ref, halo_ref, wp_ref, sc_ref, o_ref, *, bm, gdim):
    i = pl.program_id(0)
    t = lax.broadcasted_iota(jnp.int32, (bm, 1), 0) + i * bm
    for g, w in enumerate(POOL_WINDOWS):
        cols = slice(g * gdim, (g + 1) * gdim)
        u = u_ref[:, cols]
        halo = jnp.where(i > 0, halo_ref[:, cols], 0.0)
        acc = jnp.concatenate([halo, u], axis=0)
        k = 1
        while k < w:
            acc = acc + pltpu.roll(acc, k, axis=0)
            k *= 2
        count = jnp.minimum(t + 1, w).astype(F32)
        mixed = (acc[POOL_HALO:] * (1.0 / count) - u).astype(BF16)
        y = jnp.dot(mixed, wp_ref[g], preferred_element_type=F32) * sc_ref[:, cols]
        o_ref[:, cols] = y.astype(o_ref.dtype)


def multiscale_pool(u, w_pool, pool_scale, *, bm=256):
    s, width = u.shape
    ngroups, gdim, _ = w_pool.shape
    return pl.pallas_call(
        functools.partial(_pool_kernel, bm=bm, gdim=gdim),
        grid=(s // bm,),
        in_specs=[
            pl.BlockSpec((bm, width), lambda i: (i, 0)),
            pl.BlockSpec((POOL_HALO, width),
                         lambda i: (jnp.maximum(i * (bm // POOL_HALO) - 1, 0), 0)),
            pl.BlockSpec((ngroups, gdim, gdim), lambda i: (0, 0, 0)),
            pl.BlockSpec((1, width), lambda i: (0, 0)),
        ],
        out_specs=pl.BlockSpec((bm, width), lambda i: (i, 0)),
        out_shape=jax.ShapeDtypeStruct((s, width), BF16),
        compiler_params=_params(1),
        name="multiscale_pool",
    )(u, u, w_pool, pool_scale.reshape(1, width))


def _merge_kernel(h_ref, a_ref, p_ref, wga_ref, wgp_ref, wba_ref, wbp_ref, s0_ref, s1_ref,
                  o_ref, d0_ref, d1_ref):
    h = h_ref[...]
    dot = functools.partial(jnp.dot, preferred_element_type=F32)
    ya = jax.nn.sigmoid(dot(h, wga_ref[...])) * dot(a_ref[...], wba_ref[...])
    yp = jax.nn.sigmoid(dot(h, wgp_ref[...])) * dot(p_ref[...], wbp_ref[...])
    o_ref[...] = (ya + yp).astype(o_ref.dtype)
    _side_cast(s0_ref, d0_ref)
    _side_cast(s1_ref, d1_ref)


def gated_merge(h, attn, pool, w_gates, w_ba, w_bp, side_weights, *, bm, bn):
    s, d = h.shape
    ka, kp = attn.shape[1], pool.shape[1]
    nj = d // bn
    assert d % bn == 0 and s % bm == 0
    side = [_side_cast_specs(sw, (s // bm) * nj, lambda i, j: i * nj + j) for sw in side_weights]
    return pl.pallas_call(
        _merge_kernel,
        grid=(s // bm, nj),
        in_specs=[
            pl.BlockSpec((bm, d), lambda i, j: (i, 0)),
            pl.BlockSpec((bm, ka), lambda i, j: (i, 0)),
            pl.BlockSpec((bm, kp), lambda i, j: (i, 0)),
            pl.BlockSpec((d, bn), lambda i, j: (0, j)),
            pl.BlockSpec((d, bn), lambda i, j: (0, j + nj)),
            pl.BlockSpec((ka, bn), lambda i, j: (0, j)),
            pl.BlockSpec((kp, bn), lambda i, j: (0, j)),
            side[0][0], side[1][0],
        ],
        out_specs=[pl.BlockSpec((bm, bn), lambda i, j: (i, j)), side[0][1], side[1][1]],
        out_shape=[jax.ShapeDtypeStruct((s, d), BF16), side[0][2], side[1][2]],
        compiler_params=_params(2),
        name="gated_merge",
    )(h, attn, pool, w_gates, w_gates, w_ba, w_bp, *side_weights)


def _ffn_up_kernel(h_ref, wg_ref, wv_ref, cwg_ref, cwv_ref, cbg_ref, cbv_ref, wsrc_ref,
                   o_ref, wdst_ref, w_ref, u_ref, *, bm, bn, n_sub):
    i = pl.program_id(1)
    _side_cast(wsrc_ref, wdst_ref)

    @pl.when(i == 0)
    def _():
        _stage_bf16(w_ref, 0, wg_ref)
        _stage_bf16(w_ref, bn, wv_ref)
        u_ref[...] = jnp.zeros(u_ref.shape, F32)

    cw = jnp.concatenate([cwg_ref[...], cwv_ref[...]], axis=1)
    cb = jnp.concatenate([cbg_ref[...], cbv_ref[...]], axis=1)
    sub = bm // n_sub
    prev = u_ref[...]
    ups = [jnp.dot(h_ref[r * sub:(r + 1) * sub, :], w_ref[...], preferred_element_type=F32)
           for r in range(n_sub)]
    for r in range(n_sub):
        lo = r * sub
        up = ups[r]
        ext = jnp.concatenate([prev, up], axis=0)
        y = cb + up * cw[CONV_WIDTH - 1:CONV_WIDTH, :]
        for back in range(1, CONV_WIDTH):
            y = y + (pltpu.roll(ext, back, axis=0)[SUBLANES:]
                     * cw[CONV_WIDTH - 1 - back:CONV_WIDTH - back, :])
        o_ref[lo:lo + sub, :] = (jax.nn.gelu(y[:, :bn], approximate=True)
                                 * y[:, bn:]).astype(o_ref.dtype)
        prev = up[sub - SUBLANES:]
    u_ref[...] = prev


def conv_geglu_up(h, w_up, conv_w, conv_b, w_side, *, bm, bn, n_sub):
    s, d = h.shape
    d_ff = w_up.shape[1] // 2
    nj = d_ff // bn
    ni = s // bm
    assert d_ff % bn == 0 and d % CAST_ROWS == 0 and bm % n_sub == 0
    side_in, side_out, side_shape = _side_cast_specs(w_side, nj * ni, lambda j, i: j * ni + i)
    return pl.pallas_call(
        functools.partial(_ffn_up_kernel, bm=bm, bn=bn, n_sub=n_sub),
        grid=(nj, s // bm),
        in_specs=[
            pl.BlockSpec((bm, d), lambda j, i: (i, 0)),
            pl.BlockSpec((d, bn), lambda j, i: (0, j)),
            pl.BlockSpec((d, bn), lambda j, i: (0, j + nj)),
            pl.BlockSpec((CONV_WIDTH, bn), lambda j, i: (0, j)),
            pl.BlockSpec((CONV_WIDTH, bn), lambda j, i: (0, j + nj)),
            pl.BlockSpec((1, bn), lambda j, i: (0, j)),
            pl.BlockSpec((1, bn), lambda j, i: (0, j + nj)),
            side_in,
        ],
        out_specs=[pl.BlockSpec((bm, bn), lambda j, i: (i, j)), side_out],
        out_shape=[jax.ShapeDtypeStruct((s, d_ff), BF16), side_shape],
        scratch_shapes=[pltpu.VMEM((d, 2 * bn), BF16),
                        pltpu.VMEM((SUBLANES, 2 * bn), F32)],
        compiler_params=_params(2),
        name="conv_geglu_up",
    )(h, w_up, w_up, conv_w, conv_w, conv_b.reshape(1, -1), conv_b.reshape(1, -1), w_side)


def _stash_tile(t, res_ref, xo_ref, r_ref, ss_ref, *, bn):
    j = pl.program_id(1)
    xo_ref[:, pl.ds(pl.multiple_of(j * bn, bn), bn)] = t
    r_ref[j] = res_ref[...]
    ss_prev = jnp.where(j == 0, 0.0, ss_ref[...])
    ss_ref[...] = ss_prev + jnp.sum(t * t, axis=-1, keepdims=True)


def _finish_rows(r_ref, ss_ref, gp_ref, gn_ref, xo_ref, ho_ref, *, bn):
    nj = r_ref.shape[0]
    inv_d = 1.0 / (nj * bn)
    scale = lax.rsqrt(ss_ref[...] * inv_d + RMS_EPS)
    ss_new = jnp.zeros_like(scale)
    for jj in range(nj):
        cols = slice(jj * bn, (jj + 1) * bn)
        xn = r_ref[jj] + xo_ref[:, cols] * scale * gp_ref[:, cols]
        xo_ref[:, cols] = xn
        ss_new = ss_new + jnp.sum(xn * xn, axis=-1, keepdims=True)
    if ho_ref is not None:
        scale_next = lax.rsqrt(ss_new * inv_d + RMS_EPS)
        for jj in range(nj):
            cols = slice(jj * bn, (jj + 1) * bn)
            ho_ref[:, cols] = (xo_ref[:, cols] * scale_next * gn_ref[:, cols]).astype(ho_ref.dtype)


def _mm_resnorm_kernel(a_ref, w_ref, res_ref, gp_ref, gn_ref, xo_ref, ho_ref,
                       r_ref, ss_ref, *, bn):
    t = jnp.dot(a_ref[...], w_ref[...], preferred_element_type=F32)
    _stash_tile(t, res_ref, xo_ref, r_ref, ss_ref, bn=bn)

    @pl.when(pl.program_id(1) == pl.num_programs(1) - 1)
    def _():
        _finish_rows(r_ref, ss_ref, gp_ref, gn_ref, xo_ref, ho_ref, bn=bn)


def matmul_residual_norm(a, w, resid, g_post, g_next, *, bm, bn):
    s, k = a.shape
    d = w.shape[1]
    nj = d // bn
    assert s % bm == 0 and d % bn == 0
    row = pl.BlockSpec((bm, d), lambda i, j: (i, 0))
    vec = pl.BlockSpec((1, d), lambda i, j: (0, 0))
    return pl.pallas_call(
        functools.partial(_mm_resnorm_kernel, bn=bn),
        grid=(s // bm, nj),
        in_specs=[pl.BlockSpec((bm, k), lambda i, j: (i, 0)),
                  pl.BlockSpec((k, bn), lambda i, j: (0, j)),
                  pl.BlockSpec((bm, bn), lambda i, j: (i, j)),
                  vec, vec],
        out_specs=[row, row],
        out_shape=[jax.ShapeDtypeStruct((s, d), F32), jax.ShapeDtypeStruct((s, d), BF16)],
        scratch_shapes=[pltpu.VMEM((nj, bm, bn), F32), pltpu.VMEM((bm, 1), F32)],
        compiler_params=_params(2),
        name="proj_out_norm",
    )(a, w, resid, g_post.reshape(1, d), g_next.reshape(1, d))


def _ple_resnorm_kernel(h_ref, p_ref, wg_ref, wp_ref, res_ref, gp_ref, xo_ref,
                        r_ref, ss_ref, *, bn):
    g = jnp.dot(h_ref[...], wg_ref[...], preferred_element_type=F32)
    e = jnp.dot(p_ref[...].astype(BF16), wp_ref[...], preferred_element_type=F32)
    _stash_tile(jax.nn.sigmoid(g) * e, res_ref, xo_ref, r_ref, ss_ref, bn=bn)

    @pl.when(pl.program_id(1) == pl.num_programs(1) - 1)
    def _():
        _finish_rows(r_ref, ss_ref, gp_ref, None, xo_ref, None, bn=bn)


def ple_residual_norm(h, p, w_gate, w_ple, resid, g_post, *, bm, bn):
    s, d = h.shape
    pd = p.shape[1]
    nj = d // bn
    assert s % bm == 0 and d % bn == 0
    return pl.pallas_call(
        functools.partial(_ple_resnorm_kernel, bn=bn),
        grid=(s // bm, nj),
        in_specs=[
            pl.BlockSpec((bm, d), lambda i, j: (i, 0)),
            pl.BlockSpec((bm, pd), lambda i, j: (i, 0)),
            pl.BlockSpec((d, bn), lambda i, j: (0, j)),
            pl.BlockSpec((pd, bn), lambda i, j: (0, j)),
            pl.BlockSpec((bm, bn), lambda i, j: (i, j)),
            pl.BlockSpec((1, d), lambda i, j: (0, 0)),
        ],
        out_specs=pl.BlockSpec((bm, d), lambda i, j: (i, 0)),
        out_shape=jax.ShapeDtypeStruct((s, d), F32),
        scratch_shapes=[pltpu.VMEM((nj, bm, bn), F32), pltpu.VMEM((bm, 1), F32)],
        compiler_params=_params(2),
        name="ple_norm",
    )(h, p, w_gate, w_ple, resid, g_post.reshape(1, d))


def _layer(x, p, norm_mix_pre, w_in, attn_sinks, w_pool, pool_scale, w_branch_attn,
           w_branch_pool, w_out, norm_mix_post, norm_ffn_pre, w_up, conv_w, conv_b,
           w_down, norm_ffn_post, norm_ple_gate, w_ple_gate, w_ple, norm_ple_post):
    d = x.shape[1]
    attn_width = w_branch_attn.shape[0]
    pool_width = w_branch_pool.shape[0]
    qkv_width = attn_width + 2 * N_KV_HEADS * HEAD_DIM
    h = rmsnorm_cast(x, norm_mix_pre)
    qkv = matmul_wcast(h, w_in, n_out=qkv_width, col_off=0, bm=1024, bn=512,
                       out_dtype=BF16, name="proj_qkv")
    u, w_ba, w_bp = matmul_wcast(h, w_in, n_out=pool_width, col_off=qkv_width, bm=1024, bn=512,
                                 out_dtype=F32, name="proj_pool_in",
                                 side_weights=(w_branch_attn, w_branch_pool))
    attn, w_gates = sliding_window_attention(qkv, attn_sinks, w_in, 2 * d)
    pool = multiscale_pool(u, w_pool.astype(BF16), pool_scale)
    merged, w_out_b, w_ple_gate_b = gated_merge(h, attn, pool, w_gates, w_ba, w_bp,
                                                (w_out, w_ple_gate), bm=1024, bn=256)
    x, h = matmul_residual_norm(merged, w_out_b, x, norm_mix_post, norm_ffn_pre,
                                bm=512, bn=512)

    act, w_down_b = conv_geglu_up(h, w_up, conv_w, conv_b, w_down, bm=2048, bn=256, n_sub=4)
    t = matmul(act, w_down_b, n_out=d, col_off=0, bm=512, bn=512,
               out_dtype=F32, name="proj_down")
    x, h = residual_norm(x, t, norm_ffn_post, norm_ple_gate)

    return ple_residual_norm(h, p, w_ple_gate_b, w_ple.astype(BF16), x,
                             norm_ple_post, bm=512, bn=512)


def kernel(x, p, norm_mix_pre, w_in, attn_sinks, w_pool, pool_scale, w_branch_attn,
           w_branch_pool, w_out, norm_mix_post, norm_ffn_pre, w_up, conv_w, conv_b,
           w_down, norm_ffn_post, norm_ple_gate, w_ple_gate, w_ple, norm_ple_post):
    batch, seq, d = x.shape
    depth = w_in.shape[0]
    assert batch == 1
    xs = x.reshape(seq, d)
    for i in range(depth):
        xs = _layer(xs, p[i, 0], norm_mix_pre[i], w_in[i], attn_sinks[i], w_pool[i],
                    pool_scale[i], w_branch_attn[i], w_branch_pool[i], w_out[i],
                    norm_mix_post[i], norm_ffn_pre[i], w_up[i], conv_w[i], conv_b[i],
                    w_down[i], norm_ffn_post[i], norm_ple_gate[i], w_ple_gate[i],
                    w_ple[i], norm_ple_post[i])
    return xs.reshape(batch, seq, d)
```

```python
import functools

import jax
import jax.numpy as jnp
from jax import lax
from jax.experimental import pallas as pl
from jax.experimental.pallas import tpu as pltpu

F32 = jnp.float32
BF16 = jnp.bfloat16

HEAD_DIM = 64
GQA_GROUP = 8
N_KV_HEADS = 4
ATTN_BLOCK = 128
POOL_WINDOWS = (2, 4, 8, 16)
POOL_HALO = 16
CONV_WIDTH = 3
RMS_EPS = 1e-6
MASK_VALUE = -1e30

LANES = 128
SUBLANES = 8
VMEM_LIMIT_BYTES = 60 * 1024 * 1024
CAST_ROWS = 512


def _params(n_grid_dims):
    return pltpu.CompilerParams(
        dimension_semantics=("arbitrary",) * n_grid_dims,
        vmem_limit_bytes=VMEM_LIMIT_BYTES)


def _rms(x, gain):
    ms = jnp.mean(x * x, axis=-1, keepdims=True)
    return x * lax.rsqrt(ms + RMS_EPS) * gain


def _norm_kernel(x_ref, g_ref, o_ref):
    o_ref[...] = _rms(x_ref[...], g_ref[...]).astype(o_ref.dtype)


def rmsnorm_cast(x, gain, *, bm=256):
    s, d = x.shape
    return pl.pallas_call(
        _norm_kernel,
        grid=(s // bm,),
        in_specs=[pl.BlockSpec((bm, d), lambda i: (i, 0)),
                  pl.BlockSpec((1, d), lambda i: (0, 0))],
        out_specs=pl.BlockSpec((bm, d), lambda i: (i, 0)),
        out_shape=jax.ShapeDtypeStruct((s, d), BF16),
        compiler_params=_params(1),
        name="rmsnorm_cast",
    )(x, gain.reshape(1, d))


def _resnorm_kernel(x_ref, t_ref, gp_ref, gn_ref, xo_ref, ho_ref):
    xn = x_ref[...] + _rms(t_ref[...], gp_ref[...])
    xo_ref[...] = xn
    ho_ref[...] = _rms(xn, gn_ref[...]).astype(ho_ref.dtype)


def residual_norm(x, t, g_post, g_next, *, bm=256):
    s, d = x.shape
    row = pl.BlockSpec((bm, d), lambda i: (i, 0))
    vec = pl.BlockSpec((1, d), lambda i: (0, 0))
    return pl.pallas_call(
        _resnorm_kernel, grid=(s // bm,),
        in_specs=[row, row, vec, vec], out_specs=[row, row],
        out_shape=[jax.ShapeDtypeStruct((s, d), F32), jax.ShapeDtypeStruct((s, d), BF16)],
        compiler_params=_params(1), name="residual_norm",
    )(x, t, g_post.reshape(1, d), g_next.reshape(1, d))


def _mm_kernel(x_ref, w_ref, o_ref):
    o_ref[...] = jnp.dot(x_ref[...], w_ref[...],
                         preferred_element_type=F32).astype(o_ref.dtype)


def _side_cast_specs(src, n_steps, step_of, n_cols=None):
    k, width = src.shape
    n_cols = width if n_cols is None else n_cols
    rows = k // n_steps
    assert k % n_steps == 0 and rows % (2 * SUBLANES) == 0
    src_index = lambda *g: (step_of(*g) * rows, width - n_cols)
    dst_index = lambda *g: (step_of(*g), 0)
    return (pl.BlockSpec((pl.Element(rows), pl.Element(n_cols)), src_index),
            pl.BlockSpec((rows, n_cols), dst_index),
            jax.ShapeDtypeStruct((k, n_cols), BF16))


def _side_cast(src_ref, dst_ref):
    dst_ref[...] = src_ref[...].astype(BF16)


def _stage_bf16(dst_ref, col0, src_ref):
    k, n = src_ref.shape

    def cast_rows(c, carry):
        rows = pl.ds(pl.multiple_of(c * CAST_ROWS, CAST_ROWS), CAST_ROWS)
        dst_ref[rows, col0:col0 + n] = src_ref[rows, :].astype(BF16)
        return carry
    lax.fori_loop(0, k // CAST_ROWS, cast_rows, 0)


def _mm_wcast_kernel(x_ref, w_ref, *rest, n_side):
    side_src, o_ref = rest[:n_side], rest[n_side]
    side_dst, wb_ref = rest[n_side + 1:2 * n_side + 1], rest[2 * n_side + 1]

    @pl.when(pl.program_id(1) == 0)
    def _():
        _stage_bf16(wb_ref, 0, w_ref)
    o_ref[...] = jnp.dot(x_ref[...], wb_ref[...],
                         preferred_element_type=F32).astype(o_ref.dtype)
    for src_ref, dst_ref in zip(side_src, side_dst):
        _side_cast(src_ref, dst_ref)


def matmul_wcast(x, w, *, n_out, col_off, bm, bn, out_dtype, name, side_weights=()):
    m, k = x.shape
    off = col_off // bn
    assert col_off % bn == 0 and n_out % bn == 0 and m % bm == 0 and k % CAST_ROWS == 0
    ni = m // bm
    side = [_side_cast_specs(sw, (n_out // bn) * ni, lambda j, i: j * ni + i)
            for sw in side_weights]
    outs = pl.pallas_call(
        functools.partial(_mm_wcast_kernel, n_side=len(side)),
        grid=(n_out // bn, ni),
        in_specs=[pl.BlockSpec((bm, k), lambda j, i: (i, 0)),
                  pl.BlockSpec((k, bn), lambda j, i: (0, j + off))] + [sp[0] for sp in side],
        out_specs=[pl.BlockSpec((bm, bn), lambda j, i: (i, j))] + [sp[1] for sp in side],
        out_shape=[jax.ShapeDtypeStruct((m, n_out), out_dtype)] + [sp[2] for sp in side],
        scratch_shapes=[pltpu.VMEM((k, bn), BF16)],
        compiler_params=_params(2),
        name=name,
    )(x, w, *side_weights)
    return outs if side else outs[0]


def matmul(x, w, *, n_out, col_off, bm, bn, out_dtype, name):
    m, k = x.shape
    off = col_off // bn
    assert col_off % bn == 0 and n_out % bn == 0 and m % bm == 0
    return pl.pallas_call(
        _mm_kernel,
        grid=(m // bm, n_out // bn),
        in_specs=[pl.BlockSpec((bm, k), lambda i, j: (i, 0)),
                  pl.BlockSpec((k, bn), lambda i, j: (0, j + off))],
        out_specs=pl.BlockSpec((bm, bn), lambda i, j: (i, j)),
        out_shape=jax.ShapeDtypeStruct((m, n_out), out_dtype),
        compiler_params=_params(2),
        name=name,
    )(x, w)


def _attn_kernel(sink_ref, q_ref, kp_ref, kc_ref, vp_ref, vc_ref, bias_ref, wsrc_ref,
                 o_ref, wdst_ref):
    _side_cast(wsrc_ref, wdst_ref)
    hp = pl.program_id(1)
    nkeys = 2 * ATTN_BLOCK
    bias = bias_ref[...]
    lo = lax.broadcasted_iota(jnp.int32, (nkeys, LANES), 1) < HEAD_DIM
    k2 = jnp.concatenate([kp_ref[...], kc_ref[...]], axis=0).astype(F32)
    v2 = jnp.concatenate([vp_ref[...], vc_ref[...]], axis=0).astype(F32)
    k2r = pltpu.roll(k2, HEAD_DIM, axis=1)
    v2r = pltpu.roll(v2, HEAD_DIM, axis=1)
    zero = jnp.zeros_like(k2)
    lo_q = lax.broadcasted_iota(jnp.int32, (ATTN_BLOCK, LANES), 1) < HEAD_DIM
    rows_per_kv = GQA_GROUP // 2
    for hh in range(2):
        kd = jnp.where(lo, k2, k2r) if hh == 0 else jnp.where(lo, k2r, k2)
        vd = jnp.where(lo, v2, v2r) if hh == 0 else jnp.where(lo, v2r, v2)
        kbd = jnp.concatenate([jnp.where(lo, kd, zero), jnp.where(lo, zero, kd)],
                              axis=0).astype(BF16)
        vbd = jnp.concatenate([jnp.where(lo, vd, zero), jnp.where(lo, zero, vd)],
                              axis=0).astype(BF16)
        base = hh * rows_per_kv * LANES
        qs = jnp.concatenate(
            [q_ref[:, base + r * LANES: base + (r + 1) * LANES] for r in range(rows_per_kv)],
            axis=0)
        s = lax.dot_general(qs * jnp.asarray(HEAD_DIM ** -0.5, BF16), kbd,
                            (((1,), (1,)), ((), ())),
                            preferred_element_type=F32) + bias
        p_rows, inv_rows = [], []
        for r in range(rows_per_kv):
            halves, invs = [], []
            for c in range(2):
                sink = sink_ref[(2 * hp + hh) * GQA_GROUP + 2 * r + c]
                blk = s[r * ATTN_BLOCK:(r + 1) * ATTN_BLOCK, c * nkeys:(c + 1) * nkeys]
                m = jnp.maximum(jnp.max(blk, axis=-1, keepdims=True), sink)
                e = jnp.exp(blk - m)
                denom = jnp.sum(e, axis=-1, keepdims=True) + jnp.exp(sink - m)
                halves.append(e.astype(BF16))
                invs.append(1.0 / denom)
            p_rows.append(jnp.concatenate(halves, axis=1))
            inv_rows.append(jnp.where(lo_q, invs[0], invs[1]))
        p = jnp.concatenate(p_rows, axis=0)
        o = jnp.dot(p, vbd, preferred_element_type=F32)
        for r in range(rows_per_kv):
            o_ref[:, base + r * LANES: base + (r + 1) * LANES] = (
                o[r * ATTN_BLOCK:(r + 1) * ATTN_BLOCK] * inv_rows[r]).astype(o_ref.dtype)


def _attn_bias():
    nkeys = 2 * ATTN_BLOCK
    rows = (GQA_GROUP // 2) * ATTN_BLOCK
    qi = (jnp.arange(rows) % ATTN_BLOCK)[:, None]
    kj = (jnp.arange(2 * nkeys) % nkeys)[None, :]
    rel = kj - ATTN_BLOCK - qi
    in_band = (rel <= 0) & (rel > -ATTN_BLOCK)
    first = in_band & (kj >= ATTN_BLOCK)
    return jnp.where(jnp.stack([first, in_band]), 0.0, MASK_VALUE).astype(F32)


def sliding_window_attention(qkv, sinks, w_side, n_side_cols):
    s = qkv.shape[0]
    n_hp = N_KV_HEADS // 2
    side_in, side_out, side_shape = _side_cast_specs(
        w_side, (s // ATTN_BLOCK) * n_hp, lambda n, hp: n * n_hp + hp, n_side_cols)
    q_width = N_KV_HEADS * GQA_GROUP * HEAD_DIM
    pair = 2 * GQA_GROUP * HEAD_DIM
    k_blk = q_width // LANES
    v_blk = k_blk + N_KV_HEADS * HEAD_DIM // LANES
    rows = (GQA_GROUP // 2) * ATTN_BLOCK
    prev = lambda n: jnp.maximum(n - 1, 0)
    kv = (ATTN_BLOCK, LANES)
    return pl.pallas_call(
        _attn_kernel,
        grid=(s // ATTN_BLOCK, N_KV_HEADS // 2),
        in_specs=[
            pl.BlockSpec(memory_space=pltpu.SMEM),
            pl.BlockSpec((ATTN_BLOCK, pair), lambda n, hp: (n, hp)),
            pl.BlockSpec(kv, lambda n, hp: (prev(n), k_blk + hp)),
            pl.BlockSpec(kv, lambda n, hp: (n, k_blk + hp)),
            pl.BlockSpec(kv, lambda n, hp: (prev(n), v_blk + hp)),
            pl.BlockSpec(kv, lambda n, hp: (n, v_blk + hp)),
            pl.BlockSpec((None, rows, 4 * ATTN_BLOCK), lambda n, hp: (jnp.minimum(n, 1), 0, 0)),
            side_in,
        ],
        out_specs=[pl.BlockSpec((ATTN_BLOCK, pair), lambda n, hp: (n, hp)), side_out],
        out_shape=[jax.ShapeDtypeStruct((s, q_width), BF16), side_shape],
        compiler_params=_params(2),
        name="swa_attention",
    )(sinks, qkv, qkv, qkv, qkv, qkv, _attn_bias(), w_side)


def _pool_kernel(u_ref, halo_ref, wp_ref, sc_ref, o_ref, *, bm, gdim):
    i = pl.program_id(0)
    t = lax.broadcasted_iota(jnp.int32, (bm, 1), 0) + i * bm
    for g, w in enumerate(POOL_WINDOWS):
        cols = slice(g * gdim, (g + 1) * gdim)
        u = u_ref[:, cols]
        halo = jnp.where(i > 0, halo_ref[:, cols], 0.0)
        acc = jnp.concatenate([halo, u], axis=0)
        k = 1
        while k < w:
            acc = acc + pltpu.roll(acc, k, axis=0)
            k *= 2
        count = jnp.minimum(t + 1, w).astype(F32)
        mixed = (acc[POOL_HALO:] * (1.0 / count) - u).astype(BF16)
        y = jnp.dot(mixed, wp_ref[g], preferred_element_type=F32) * sc_ref[:, cols]
        o_ref[:, cols] = y.astype(o_ref.dtype)


def multiscale_pool(u, w_pool, pool_scale, *, bm=256):
    s, width = u.shape
    ngroups, gdim, _ = w_pool.shape
    return pl.pallas_call(
        functools.partial(_pool_kernel, bm=bm, gdim=gdim),
        grid=(s // bm,),
        in_specs=[
            pl.BlockSpec((bm, width), lambda i: (i, 0)),
            pl.BlockSpec((POOL_HALO, width),
                         lambda i: (jnp.maximum(i * (bm // POOL_HALO) - 1, 0), 0)),
            pl.BlockSpec((ngroups, gdim, gdim), lambda i: (0, 0, 0)),
            pl.BlockSpec((1, width), lambda i: (0, 0)),
        ],
        out_specs=pl.BlockSpec((bm, width), lambda i: (i, 0)),
        out_shape=jax.ShapeDtypeStruct((s, width), BF16),
        compiler_params=_params(1),
        name="multiscale_pool",
    )(u, u, w_pool, pool_scale.reshape(1, width))


def _merge_kernel(h_ref, a_ref, p_ref, wga_ref, wgp_ref, wba_ref, wbp_ref, s0_ref, s1_ref,
                  o_ref, d0_ref, d1_ref):
    h = h_ref[...]
    dot = functools.partial(jnp.dot, preferred_element_type=F32)
    ya = jax.nn.sigmoid(dot(h, wga_ref[...])) * dot(a_ref[...], wba_ref[...])
    yp = jax.nn.sigmoid(dot(h, wgp_ref[...])) * dot(p_ref[...], wbp_ref[...])
    o_ref[...] = (ya + yp).astype(o_ref.dtype)
    _side_cast(s0_ref, d0_ref)
    _side_cast(s1_ref, d1_ref)


def gated_merge(h, attn, pool, w_gates, w_ba, w_bp, side_weights, *, bm, bn):
    s, d = h.shape
    ka, kp = attn.shape[1], pool.shape[1]
    nj = d // bn
    assert d % bn == 0 and s % bm == 0
    side = [_side_cast_specs(sw, (s // bm) * nj, lambda i, j: i * nj + j) for sw in side_weights]
    return pl.pallas_call(
        _merge_kernel,
        grid=(s // bm, nj),
        in_specs=[
            pl.BlockSpec((bm, d), lambda i, j: (i, 0)),
            pl.BlockSpec((bm, ka), lambda i, j: (i, 0)),
            pl.BlockSpec((bm, kp), lambda i, j: (i, 0)),
            pl.BlockSpec((d, bn), lambda i, j: (0, j)),
            pl.BlockSpec((d, bn), lambda i, j: (0, j + nj)),
            pl.BlockSpec((ka, bn), lambda i, j: (0, j)),
            pl.BlockSpec((kp, bn), lambda i, j: (0, j)),
            side[0][0], side[1][0],
        ],
        out_specs=[pl.BlockSpec((bm, bn), lambda i, j: (i, j)), side[0][1], side[1][1]],
        out_shape=[jax.ShapeDtypeStruct((s, d), BF16), side[0][2], side[1][2]],
        compiler_params=_params(2),
        name="gated_merge",
    )(h, attn, pool, w_gates, w_gates, w_ba, w_bp, *side_weights)


def _ffn_up_kernel(h_ref, wg_ref, wv_ref, cwg_ref, cwv_ref, cbg_ref, cbv_ref, wsrc_ref,
                   o_ref, wdst_ref, w_ref, u_ref, *, bm, bn, n_sub):
    i = pl.program_id(1)
    _side_cast(wsrc_ref, wdst_ref)

    @pl.when(i == 0)
    def _():
        _stage_bf16(w_ref, 0, wg_ref)
        _stage_bf16(w_ref, bn, wv_ref)
        u_ref[...] = jnp.zeros(u_ref.shape, F32)

    cw = jnp.concatenate([cwg_ref[...], cwv_ref[...]], axis=1)
    cb = jnp.concatenate([cbg_ref[...], cbv_ref[...]], axis=1)
    sub = bm // n_sub
    prev = u_ref[...]
    ups = [jnp.dot(h_ref[r * sub:(r + 1) * sub, :], w_ref[...], preferred_element_type=F32)
           for r in range(n_sub)]
    for r in range(n_sub):
        lo = r * sub
        up = ups[r]
        ext = jnp.concatenate([prev, up], axis=0)
        y = cb + up * cw[CONV_WIDTH - 1:CONV_WIDTH, :]
        for back in range(1, CONV_WIDTH):
            y = y + (pltpu.roll(ext, back, axis=0)[SUBLANES:]
                     * cw[CONV_WIDTH - 1 - back:CONV_WIDTH - back, :])
        o_ref[lo:lo + sub, :] = (jax.nn.gelu(y[:, :bn], approximate=True)
                                 * y[:, bn:]).astype(o_ref.dtype)
        prev = up[sub - SUBLANES:]
    u_ref[...] = prev


def conv_geglu_up(h, w_up, conv_w, conv_b, w_side, *, bm, bn, n_sub):
    s, d = h.shape
    d_ff = w_up.shape[1] // 2
    nj = d_ff // bn
    ni = s // bm
    assert d_ff % bn == 0 and d % CAST_ROWS == 0 and bm % n_sub == 0
    side_in, side_out, side_shape = _side_cast_specs(w_side, nj * ni, lambda j, i: j * ni + i)
    return pl.pallas_call(
        functools.partial(_ffn_up_kernel, bm=bm, bn=bn, n_sub=n_sub),
        grid=(nj, s // bm),
        in_specs=[
            pl.BlockSpec((bm, d), lambda j, i: (i, 0)),
            pl.BlockSpec((d, bn), lambda j, i: (0, j)),
            pl.BlockSpec((d, bn), lambda j, i: (0, j + nj)),
            pl.BlockSpec((CONV_WIDTH, bn), lambda j, i: (0, j)),
            pl.BlockSpec((CONV_WIDTH, bn), lambda j, i: (0, j + nj)),
            pl.BlockSpec((1, bn), lambda j, i: (0, j)),
            pl.BlockSpec((1, bn), lambda j, i: (0, j + nj)),
            side_in,
        ],
        out_specs=[pl.BlockSpec((bm, bn), lambda j, i: (i, j)), side_out],
        out_shape=[jax.ShapeDtypeStruct((s, d_ff), BF16), side_shape],
        scratch_shapes=[pltpu.VMEM((d, 2 * bn), BF16),
                        pltpu.VMEM((SUBLANES, 2 * bn), F32)],
        compiler_params=_params(2),
        name="conv_geglu_up",
    )(h, w_up, w_up, conv_w, conv_w, conv_b.reshape(1, -1), conv_b.reshape(1, -1), w_side)


def _stash_tile(t, res_ref, xo_ref, r_ref, ss_ref, *, bn):
    j = pl.program_id(1)
    xo_ref[:, pl.ds(pl.multiple_of(j * bn, bn), bn)] = t
    r_ref[j] = res_ref[...]
    ss_prev = jnp.where(j == 0, 0.0, ss_ref[...])
    ss_ref[...] = ss_prev + jnp.sum(t * t, axis=-1, keepdims=True)


def _finish_rows(r_ref, ss_ref, gp_ref, gn_ref, xo_ref, ho_ref, *, bn):
    nj = r_ref.shape[0]
    inv_d = 1.0 / (nj * bn)
    scale = lax.rsqrt(ss_ref[...] * inv_d + RMS_EPS)
    ss_new = jnp.zeros_like(scale)
    for jj in range(nj):
        cols = slice(jj * bn, (jj + 1) * bn)
        xn = r_ref[jj] + xo_ref[:, cols] * scale * gp_ref[:, cols]
        xo_ref[:, cols] = xn
        ss_new = ss_new + jnp.sum(xn * xn, axis=-1, keepdims=True)
    if ho_ref is not None:
        scale_next = lax.rsqrt(ss_new * inv_d + RMS_EPS)
        for jj in range(nj):
            cols = slice(jj * bn, (jj + 1) * bn)
            ho_ref[:, cols] = (xo_ref[:, cols] * scale_next * gn_ref[:, cols]).astype(ho_ref.dtype)


def _mm_resnorm_kernel(a_ref, w_ref, res_ref, gp_ref, gn_ref, xo_ref, ho_ref,
                       r_ref, ss_ref, *, bn):
    t = jnp.dot(a_ref[...], w_ref[...], preferred_element_type=F32)
    _stash_tile(t, res_ref, xo_ref, r_ref, ss_ref, bn=bn)

    @pl.when(pl.program_id(1) == pl.num_programs(1) - 1)
    def _():
        _finish_rows(r_ref, ss_ref, gp_ref, gn_ref, xo_ref, ho_ref, bn=bn)


def matmul_residual_norm(a, w, resid, g_post, g_next, *, bm, bn):
    s, k = a.shape
    d = w.shape[1]
    nj = d // bn
    assert s % bm == 0 and d % bn == 0
    row = pl.BlockSpec((bm, d), lambda i, j: (i, 0))
    vec = pl.BlockSpec((1, d), lambda i, j: (0, 0))
    return pl.pallas_call(
        functools.partial(_mm_resnorm_kernel, bn=bn),
        grid=(s // bm, nj),
        in_specs=[pl.BlockSpec((bm, k), lambda i, j: (i, 0)),
                  pl.BlockSpec((k, bn), lambda i, j: (0, j)),
                  pl.BlockSpec((bm, bn), lambda i, j: (i, j)),
                  vec, vec],
        out_specs=[row, row],
        out_shape=[jax.ShapeDtypeStruct((s, d), F32), jax.ShapeDtypeStruct((s, d), BF16)],
        scratch_shapes=[pltpu.VMEM((nj, bm, bn), F32), pltpu.VMEM((bm, 1), F32)],
        compiler_params=_params(2),
        name="proj_out_norm",
    )(a, w, resid, g_post.reshape(1, d), g_next.reshape(1, d))


def _ple_resnorm_kernel(h_ref, p_ref, wg_ref, wp_ref, res_ref, gp_ref, xo_ref,
                        t_ref, ss_ref, *, ni, nj, bn):
    i, j = pl.program_id(0), pl.program_id(1)
    cur, done = 0, 1

    @pl.when(i > 0)
    def _():
        scale = lax.rsqrt(ss_ref[done] * (1.0 / (nj * bn)) + RMS_EPS)
        xo_ref[...] = res_ref[...] + t_ref[j] * scale * gp_ref[...]

    @pl.when(i < ni)
    def _():
        g = jnp.dot(h_ref[...], wg_ref[...], preferred_element_type=F32)
        e = jnp.dot(p_ref[...].astype(BF16), wp_ref[...], preferred_element_type=F32)
        t = jax.nn.sigmoid(g) * e
        t_ref[j] = t
        ss = jnp.where(j == 0, 0.0, ss_ref[cur]) + jnp.sum(t * t, axis=-1, keepdims=True)
        ss_ref[cur] = ss

        @pl.when(j == nj - 1)
        def _():
            ss_ref[done] = ss


def ple_residual_norm(h, p, w_gate, w_ple, resid, g_post, *, bm, bn):
    s, d = h.shape
    pd = p.shape[1]
    ni, nj = s // bm, d // bn
    assert s % bm == 0 and d % bn == 0
    mm_row = lambda i, j: (jnp.minimum(i, ni - 1), 0)
    out_tile = lambda i, j: (jnp.maximum(i - 1, 0), jnp.where(i == 0, 0, j))
    return pl.pallas_call(
        functools.partial(_ple_resnorm_kernel, ni=ni, nj=nj, bn=bn),
        grid=(ni + 1, nj),
        in_specs=[
            pl.BlockSpec((bm, d), mm_row),
            pl.BlockSpec((bm, pd), mm_row),
            pl.BlockSpec((d, bn), lambda i, j: (0, j)),
            pl.BlockSpec((pd, bn), lambda i, j: (0, j)),
            pl.BlockSpec((bm, bn), out_tile),
            pl.BlockSpec((1, bn), lambda i, j: (0, j)),
        ],
        out_specs=pl.BlockSpec((bm, bn), out_tile),
        out_shape=jax.ShapeDtypeStruct((s, d), F32),
        scratch_shapes=[pltpu.VMEM((nj, bm, bn), F32), pltpu.VMEM((2, bm, 1), F32)],
        compiler_params=_params(2),
        name="ple_norm",
    )(h, p, w_gate, w_ple, resid, g_post.reshape(1, d))


def _layer(x, p, norm_mix_pre, w_in, attn_sinks, w_pool, pool_scale, w_branch_attn,
           w_branch_pool, w_out, norm_mix_post, norm_ffn_pre, w_up, conv_w, conv_b,
           w_down, norm_ffn_post, norm_ple_gate, w_ple_gate, w_ple, norm_ple_post):
    d = x.shape[1]
    attn_width = w_branch_attn.shape[0]
    pool_width = w_branch_pool.shape[0]
    qkv_width = attn_width + 2 * N_KV_HEADS * HEAD_DIM
    h = rmsnorm_cast(x, norm_mix_pre)
    qkv = matmul_wcast(h, w_in, n_out=qkv_width, col_off=0, bm=1024, bn=512,
                       out_dtype=BF16, name="proj_qkv")
    u, w_ba, w_bp = matmul_wcast(h, w_in, n_out=pool_width, col_off=qkv_width, bm=1024, bn=512,
                                 out_dtype=F32, name="proj_pool_in",
                                 side_weights=(w_branch_attn, w_branch_pool))
    attn, w_gates = sliding_window_attention(qkv, attn_sinks, w_in, 2 * d)
    pool = multiscale_pool(u, w_pool.astype(BF16), pool_scale)
    merged, w_out_b, w_ple_gate_b = gated_merge(h, attn, pool, w_gates, w_ba, w_bp,
                                                (w_out, w_ple_gate), bm=1024, bn=256)
    x, h = matmul_residual_norm(merged, w_out_b, x, norm_mix_post, norm_ffn_pre,
                                bm=512, bn=512)

    act, w_down_b = conv_geglu_up(h, w_up, conv_w, conv_b, w_down, bm=2048, bn=256, n_sub=4)
    t = matmul(act, w_down_b, n_out=d, col_off=0, bm=512, bn=512,
               out_dtype=F32, name="proj_down")
    x, h = residual_norm(x, t, norm_ffn_post, norm_ple_gate)

    return ple_residual_norm(h, p, w_ple_gate_b, w_ple.astype(BF16), x,
                             norm_ple_post, bm=1024, bn=512)


def kernel(x, p, norm_mix_pre, w_in, attn_sinks, w_pool, pool_scale, w_branch_attn,
           w_branch_pool, w_out, norm_mix_post, norm_ffn_pre, w_up, conv_w, conv_b,
           w_down, norm_ffn_post, norm_ple_gate, w_ple_gate, w_ple, norm_ple_post):
    batch, seq, d = x.shape
    depth = w_in.shape[0]
    assert batch == 1
    xs = x.reshape(seq, d)
    for i in range(depth):
        xs = _layer(xs, p[i, 0], norm_mix_pre[i], w_in[i], attn_sinks[i], w_pool[i],
                    pool_scale[i], w_branch_attn[i], w_branch_pool[i], w_out[i],
                    norm_mix_post[i], norm_ffn_pre[i], w_up[i], conv_w[i], conv_b[i],
                    w_down[i], norm_ffn_post[i], norm_ple_gate[i], w_ple_gate[i],
                    w_ple[i], norm_ple_post[i])
    return xs.reshape(batch, seq, d)
```

```python
import functools

import jax
import jax.numpy as jnp
from jax import lax
from jax.experimental import pallas as pl
from jax.experimental.pallas import tpu as pltpu

F32 = jnp.float32
BF16 = jnp.bfloat16

HEAD_DIM = 64
GQA_GROUP = 8
N_KV_HEADS = 4
ATTN_BLOCK = 128
ATTN_Q_BLOCKS = 2
POOL_WINDOWS = (2, 4, 8, 16)
POOL_HALO = 16
CONV_WIDTH = 3
RMS_EPS = 1e-6
MASK_VALUE = -1e30

LANES = 128
SUBLANES = 8
VMEM_LIMIT_BYTES = 60 * 1024 * 1024
CAST_ROWS = 512


def _params(n_grid_dims):
    return pltpu.CompilerParams(
        dimension_semantics=("arbitrary",) * n_grid_dims,
        vmem_limit_bytes=VMEM_LIMIT_BYTES)


def _rms(x, gain):
    ms = jnp.mean(x * x, axis=-1, keepdims=True)
    return x * lax.rsqrt(ms + RMS_EPS) * gain


def _norm_kernel(x_ref, g_ref, o_ref):
    o_ref[...] = _rms(x_ref[...], g_ref[...]).astype(o_ref.dtype)


def rmsnorm_cast(x, gain, *, bm=256):
    s, d = x.shape
    return pl.pallas_call(
        _norm_kernel,
        grid=(s // bm,),
        in_specs=[pl.BlockSpec((bm, d), lambda i: (i, 0)),
                  pl.BlockSpec((1, d), lambda i: (0, 0))],
        out_specs=pl.BlockSpec((bm, d), lambda i: (i, 0)),
        out_shape=jax.ShapeDtypeStruct((s, d), BF16),
        compiler_params=_params(1),
        name="rmsnorm_cast",
    )(x, gain.reshape(1, d))


def _resnorm_kernel(x_ref, t_ref, gp_ref, gn_ref, ho_ref, sc_ref):
    t = t_ref[...]
    scale = lax.rsqrt(jnp.mean(t * t, axis=-1, keepdims=True) + RMS_EPS)
    sc_ref[...] = scale
    xn = x_ref[...] + t * scale * gp_ref[...]
    ho_ref[...] = _rms(xn, gn_ref[...]).astype(ho_ref.dtype)


def residual_norm(x, t, g_post, g_next, *, bm=256):
    s, d = x.shape
    row = pl.BlockSpec((bm, d), lambda i: (i, 0))
    vec = pl.BlockSpec((1, d), lambda i: (0, 0))
    return pl.pallas_call(
        _resnorm_kernel, grid=(s // bm,),
        in_specs=[row, row, vec, vec],
        out_specs=[row, pl.BlockSpec((bm, 1), lambda i: (i, 0))],
        out_shape=[jax.ShapeDtypeStruct((s, d), BF16), jax.ShapeDtypeStruct((s, 1), F32)],
        compiler_params=_params(1), name="residual_norm",
    )(x, t, g_post.reshape(1, d), g_next.reshape(1, d))


def _mm_kernel(x_ref, w_ref, o_ref):
    o_ref[...] = jnp.dot(x_ref[...], w_ref[...],
                         preferred_element_type=F32).astype(o_ref.dtype)


def _side_cast_specs(src, n_steps, step_of, n_cols=None):
    k, width = src.shape
    n_cols = width if n_cols is None else n_cols
    rows = k // n_steps
    assert k % n_steps == 0 and rows % (2 * SUBLANES) == 0
    src_index = lambda *g: (step_of(*g) * rows, width - n_cols)
    dst_index = lambda *g: (step_of(*g), 0)
    return (pl.BlockSpec((pl.Element(rows), pl.Element(n_cols)), src_index),
            pl.BlockSpec((rows, n_cols), dst_index),
            jax.ShapeDtypeStruct((k, n_cols), BF16))


def _side_cast(src_ref, dst_ref):
    dst_ref[...] = src_ref[...].astype(BF16)


def _stage_bf16(dst_ref, col0, src_ref):
    k, n = src_ref.shape

    def cast_rows(c, carry):
        rows = pl.ds(pl.multiple_of(c * CAST_ROWS, CAST_ROWS), CAST_ROWS)
        dst_ref[rows, col0:col0 + n] = src_ref[rows, :].astype(BF16)
        return carry
    lax.fori_loop(0, k // CAST_ROWS, cast_rows, 0)


def _mm_wcast_kernel(x_ref, w_ref, *rest, n_side):
    side_src, o_ref = rest[:n_side], rest[n_side]
    side_dst, wb_ref = rest[n_side + 1:2 * n_side + 1], rest[2 * n_side + 1]

    @pl.when(pl.program_id(1) == 0)
    def _():
        _stage_bf16(wb_ref, 0, w_ref)
    o_ref[...] = jnp.dot(x_ref[...], wb_ref[...],
                         preferred_element_type=F32).astype(o_ref.dtype)
    for src_ref, dst_ref in zip(side_src, side_dst):
        _side_cast(src_ref, dst_ref)


def matmul_wcast(x, w, *, n_out, col_off, bm, bn, out_dtype, name, side_weights=()):
    m, k = x.shape
    off = col_off // bn
    assert col_off % bn == 0 and n_out % bn == 0 and m % bm == 0 and k % CAST_ROWS == 0
    ni = m // bm
    side = [_side_cast_specs(sw, (n_out // bn) * ni, lambda j, i: j * ni + i)
            for sw in side_weights]
    outs = pl.pallas_call(
        functools.partial(_mm_wcast_kernel, n_side=len(side)),
        grid=(n_out // bn, ni),
        in_specs=[pl.BlockSpec((bm, k), lambda j, i: (i, 0)),
                  pl.BlockSpec((k, bn), lambda j, i: (0, j + off))] + [sp[0] for sp in side],
        out_specs=[pl.BlockSpec((bm, bn), lambda j, i: (i, j))] + [sp[1] for sp in side],
        out_shape=[jax.ShapeDtypeStruct((m, n_out), out_dtype)] + [sp[2] for sp in side],
        scratch_shapes=[pltpu.VMEM((k, bn), BF16)],
        compiler_params=_params(2),
        name=name,
    )(x, w, *side_weights)
    return outs if side else outs[0]


def matmul(x, w, *, n_out, col_off, bm, bn, out_dtype, name):
    m, k = x.shape
    off = col_off // bn
    assert col_off % bn == 0 and n_out % bn == 0 and m % bm == 0
    return pl.pallas_call(
        _mm_kernel,
        grid=(m // bm, n_out // bn),
        in_specs=[pl.BlockSpec((bm, k), lambda i, j: (i, 0)),
                  pl.BlockSpec((k, bn), lambda i, j: (0, j + off))],
        out_specs=pl.BlockSpec((bm, bn), lambda i, j: (i, j)),
        out_shape=jax.ShapeDtypeStruct((m, n_out), out_dtype),
        compiler_params=_params(2),
        name=name,
    )(x, w)


def _attn_kernel(sink_ref, q_ref, kp_ref, kc_ref, vp_ref, vc_ref, bias_ref, wsrc_ref,
                 o_ref, wdst_ref):
    _side_cast(wsrc_ref, wdst_ref)
    n, hp = pl.program_id(0), pl.program_id(1)
    nkeys = 2 * ATTN_BLOCK
    lo = lax.broadcasted_iota(jnp.int32, (nkeys, LANES), 1) < HEAD_DIM
    lo_q = lax.broadcasted_iota(jnp.int32, (ATTN_BLOCK, LANES), 1) < HEAD_DIM
    k_all = jnp.concatenate([kp_ref[...], kc_ref[...]], axis=0).astype(F32)
    v_all = jnp.concatenate([vp_ref[...], vc_ref[...]], axis=0).astype(F32)
    k_all_r = pltpu.roll(k_all, HEAD_DIM, axis=1)
    v_all_r = pltpu.roll(v_all, HEAD_DIM, axis=1)
    zero = jnp.zeros((nkeys, LANES), F32)
    rows_per_kv = GQA_GROUP // 2
    for qb in range(ATTN_Q_BLOCKS):
        qrows = slice(qb * ATTN_BLOCK, (qb + 1) * ATTN_BLOCK)
        krows = slice(qb * ATTN_BLOCK, qb * ATTN_BLOCK + nkeys)
        k2, k2r, v2, v2r = k_all[krows], k_all_r[krows], v_all[krows], v_all_r[krows]
        bias = bias_ref[jnp.minimum(n, 1)] if qb == 0 else bias_ref[1]
        for hh in range(2):
            kd = jnp.where(lo, k2, k2r) if hh == 0 else jnp.where(lo, k2r, k2)
            vd = jnp.where(lo, v2, v2r) if hh == 0 else jnp.where(lo, v2r, v2)
            kbd = jnp.concatenate([jnp.where(lo, kd, zero), jnp.where(lo, zero, kd)],
                                  axis=0).astype(BF16)
            vbd = jnp.concatenate([jnp.where(lo, vd, zero), jnp.where(lo, zero, vd)],
                                  axis=0).astype(BF16)
            base = hh * rows_per_kv * LANES
            qs = jnp.concatenate(
                [q_ref[qrows, base + r * LANES: base + (r + 1) * LANES]
                 for r in range(rows_per_kv)], axis=0)
            s = lax.dot_general(qs * jnp.asarray(HEAD_DIM ** -0.5, BF16), kbd,
                                (((1,), (1,)), ((), ())),
                                preferred_element_type=F32) + bias
            p_rows, inv_rows = [], []
            for r in range(rows_per_kv):
                halves, invs = [], []
                for c in range(2):
                    sink = sink_ref[(2 * hp + hh) * GQA_GROUP + 2 * r + c]
                    blk = s[r * ATTN_BLOCK:(r + 1) * ATTN_BLOCK, c * nkeys:(c + 1) * nkeys]
                    m = jnp.maximum(jnp.max(blk, axis=-1, keepdims=True), sink)
                    e = jnp.exp(blk - m)
                    denom = jnp.sum(e, axis=-1, keepdims=True) + jnp.exp(sink - m)
                    halves.append(e.astype(BF16))
                    invs.append(1.0 / denom)
                p_rows.append(jnp.concatenate(halves, axis=1))
                inv_rows.append(jnp.where(lo_q, invs[0], invs[1]))
            p = jnp.concatenate(p_rows, axis=0)
            o = jnp.dot(p, vbd, preferred_element_type=F32)
            for r in range(rows_per_kv):
                o_ref[qrows, base + r * LANES: base + (r + 1) * LANES] = (
                    o[r * ATTN_BLOCK:(r + 1) * ATTN_BLOCK] * inv_rows[r]).astype(o_ref.dtype)


def _attn_bias():
    nkeys = 2 * ATTN_BLOCK
    rows = (GQA_GROUP // 2) * ATTN_BLOCK
    qi = (jnp.arange(rows) % ATTN_BLOCK)[:, None]
    kj = (jnp.arange(2 * nkeys) % nkeys)[None, :]
    rel = kj - ATTN_BLOCK - qi
    in_band = (rel <= 0) & (rel > -ATTN_BLOCK)
    first = in_band & (kj >= ATTN_BLOCK)
    return jnp.where(jnp.stack([first, in_band]), 0.0, MASK_VALUE).astype(F32)


def sliding_window_attention(qkv, sinks, w_side, n_side_cols):
    s = qkv.shape[0]
    n_hp = N_KV_HEADS // 2
    q_rows = ATTN_Q_BLOCKS * ATTN_BLOCK
    assert s % q_rows == 0
    side_in, side_out, side_shape = _side_cast_specs(
        w_side, (s // q_rows) * n_hp, lambda n, hp: n * n_hp + hp, n_side_cols)
    q_width = N_KV_HEADS * GQA_GROUP * HEAD_DIM
    pair = 2 * GQA_GROUP * HEAD_DIM
    k_blk = q_width // LANES
    v_blk = k_blk + N_KV_HEADS * HEAD_DIM // LANES
    rows = (GQA_GROUP // 2) * ATTN_BLOCK
    prev = lambda n: jnp.maximum(n * ATTN_Q_BLOCKS - 1, 0)
    kv_prev, kv_cur = (ATTN_BLOCK, LANES), (q_rows, LANES)
    return pl.pallas_call(
        _attn_kernel,
        grid=(s // q_rows, n_hp),
        in_specs=[
            pl.BlockSpec(memory_space=pltpu.SMEM),
            pl.BlockSpec((q_rows, pair), lambda n, hp: (n, hp)),
            pl.BlockSpec(kv_prev, lambda n, hp: (prev(n), k_blk + hp)),
            pl.BlockSpec(kv_cur, lambda n, hp: (n, k_blk + hp)),
            pl.BlockSpec(kv_prev, lambda n, hp: (prev(n), v_blk + hp)),
            pl.BlockSpec(kv_cur, lambda n, hp: (n, v_blk + hp)),
            pl.BlockSpec((2, rows, 4 * ATTN_BLOCK), lambda n, hp: (0, 0, 0)),
            side_in,
        ],
        out_specs=[pl.BlockSpec((q_rows, pair), lambda n, hp: (n, hp)), side_out],
        out_shape=[jax.ShapeDtypeStruct((s, q_width), BF16), side_shape],
        compiler_params=_params(2),
        name="swa_attention",
    )(sinks, qkv, qkv, qkv, qkv, qkv, _attn_bias(), w_side)


def _pool_kernel(u_ref, halo_ref, wp_ref, sc_ref, o_ref, *, bm, gdim):
    i = pl.program_id(0)
    t = lax.broadcasted_iota(jnp.int32, (bm, 1), 0) + i * bm
    for g, w in enumerate(POOL_WINDOWS):
        cols = slice(g * gdim, (g + 1) * gdim)
        u = u_ref[:, cols]
        halo = jnp.where(i > 0, halo_ref[:, cols], 0.0)
        acc = jnp.concatenate([halo, u], axis=0)
        k = 1
        while k < w:
            acc = acc + pltpu.roll(acc, k, axis=0)
            k *= 2
        count = jnp.minimum(t + 1, w).astype(F32)
        mixed = (acc[POOL_HALO:] * (1.0 / count) - u).astype(BF16)
        y = jnp.dot(mixed, wp_ref[g], preferred_element_type=F32) * sc_ref[:, cols]
        o_ref[:, cols] = y.astype(o_ref.dtype)


def multiscale_pool(u, w_pool, pool_scale, *, bm=256):
    s, width = u.shape
    ngroups, gdim, _ = w_pool.shape
    return pl.pallas_call(
        functools.partial(_pool_kernel, bm=bm, gdim=gdim),
        grid=(s // bm,),
        in_specs=[
            pl.BlockSpec((bm, width), lambda i: (i, 0)),
            pl.BlockSpec((POOL_HALO, width),
                         lambda i: (jnp.maximum(i * (bm // POOL_HALO) - 1, 0), 0)),
            pl.BlockSpec((ngroups, gdim, gdim), lambda i: (0, 0, 0)),
            pl.BlockSpec((1, width), lambda i: (0, 0)),
        ],
        out_specs=pl.BlockSpec((bm, width), lambda i: (i, 0)),
        out_shape=jax.ShapeDtypeStruct((s, width), BF16),
        compiler_params=_params(1),
        name="multiscale_pool",
    )(u, u, w_pool, pool_scale.reshape(1, width))


def _merge_kernel(h_ref, a_ref, p_ref, wga_ref, wgp_ref, wba_ref, wbp_ref, s0_ref, s1_ref,
                  o_ref, d0_ref, d1_ref):
    h = h_ref[...]
    dot = functools.partial(jnp.dot, preferred_element_type=F32)
    ya = jax.nn.sigmoid(dot(h, wga_ref[...])) * dot(a_ref[...], wba_ref[...])
    yp = jax.nn.sigmoid(dot(h, wgp_ref[...])) * dot(p_ref[...], wbp_ref[...])
    o_ref[...] = (ya + yp).astype(o_ref.dtype)
    _side_cast(s0_ref, d0_ref)
    _side_cast(s1_ref, d1_ref)


def gated_merge(h, attn, pool, w_gates, w_ba, w_bp, side_weights, *, bm, bn):
    s, d = h.shape
    ka, kp = attn.shape[1], pool.shape[1]
    nj = d // bn
    assert d % bn == 0 and s % bm == 0
    side = [_side_cast_specs(sw, (s // bm) * nj, lambda i, j: i * nj + j) for sw in side_weights]
    return pl.pallas_call(
        _merge_kernel,
        grid=(s // bm, nj),
        in_specs=[
            pl.BlockSpec((bm, d), lambda i, j: (i, 0)),
            pl.BlockSpec((bm, ka), lambda i, j: (i, 0)),
            pl.BlockSpec((bm, kp), lambda i, j: (i, 0)),
            pl.BlockSpec((d, bn), lambda i, j: (0, j)),
            pl.BlockSpec((d, bn), lambda i, j: (0, j + nj)),
            pl.BlockSpec((ka, bn), lambda i, j: (0, j)),
            pl.BlockSpec((kp, bn), lambda i, j: (0, j)),
            side[0][0], side[1][0],
        ],
        out_specs=[pl.BlockSpec((bm, bn), lambda i, j: (i, j)), side[0][1], side[1][1]],
        out_shape=[jax.ShapeDtypeStruct((s, d), BF16), side[0][2], side[1][2]],
        compiler_params=_params(2),
        name="gated_merge",
    )(h, attn, pool, w_gates, w_gates, w_ba, w_bp, *side_weights)


def _ffn_up_kernel(h_ref, wg_ref, wv_ref, cwg_ref, cwv_ref, cbg_ref, cbv_ref, wsrc_ref,
                   o_ref, wdst_ref, w_ref, u_ref, *, bm, bn, n_sub):
    i = pl.program_id(1)
    _side_cast(wsrc_ref, wdst_ref)

    @pl.when(i == 0)
    def _():
        _stage_bf16(w_ref, 0, wg_ref)
        _stage_bf16(w_ref, bn, wv_ref)
        u_ref[...] = jnp.zeros(u_ref.shape, F32)

    cw = jnp.concatenate([cwg_ref[...], cwv_ref[...]], axis=1)
    cb = jnp.concatenate([cbg_ref[...], cbv_ref[...]], axis=1)
    sub = bm // n_sub
    prev = u_ref[...]
    ups = [jnp.dot(h_ref[r * sub:(r + 1) * sub, :], w_ref[...], preferred_element_type=F32)
           for r in range(n_sub)]
    for r in range(n_sub):
        lo = r * sub
        up = ups[r]
        ext = jnp.concatenate([prev, up], axis=0)
        y = cb + up * cw[CONV_WIDTH - 1:CONV_WIDTH, :]
        for back in range(1, CONV_WIDTH):
            y = y + (pltpu.roll(ext, back, axis=0)[SUBLANES:]
                     * cw[CONV_WIDTH - 1 - back:CONV_WIDTH - back, :])
        o_ref[lo:lo + sub, :] = (jax.nn.gelu(y[:, :bn], approximate=True)
                                 * y[:, bn:]).astype(o_ref.dtype)
        prev = up[sub - SUBLANES:]
    u_ref[...] = prev


def conv_geglu_up(h, w_up, conv_w, conv_b, w_side, *, bm, bn, n_sub):
    s, d = h.shape
    d_ff = w_up.shape[1] // 2
    nj = d_ff // bn
    ni = s // bm
    assert d_ff % bn == 0 and d % CAST_ROWS == 0 and bm % n_sub == 0
    side_in, side_out, side_shape = _side_cast_specs(w_side, nj * ni, lambda j, i: j * ni + i)
    return pl.pallas_call(
        functools.partial(_ffn_up_kernel, bm=bm, bn=bn, n_sub=n_sub),
        grid=(nj, s // bm),
        in_specs=[
            pl.BlockSpec((bm, d), lambda j, i: (i, 0)),
            pl.BlockSpec((d, bn), lambda j, i: (0, j)),
            pl.BlockSpec((d, bn), lambda j, i: (0, j + nj)),
            pl.BlockSpec((CONV_WIDTH, bn), lambda j, i: (0, j)),
            pl.BlockSpec((CONV_WIDTH, bn), lambda j, i: (0, j + nj)),
            pl.BlockSpec((1, bn), lambda j, i: (0, j)),
            pl.BlockSpec((1, bn), lambda j, i: (0, j + nj)),
            side_in,
        ],
        out_specs=[pl.BlockSpec((bm, bn), lambda j, i: (i, j)), side_out],
        out_shape=[jax.ShapeDtypeStruct((s, d_ff), BF16), side_shape],
        scratch_shapes=[pltpu.VMEM((d, 2 * bn), BF16),
                        pltpu.VMEM((SUBLANES, 2 * bn), F32)],
        compiler_params=_params(2),
        name="conv_geglu_up",
    )(h, w_up, w_up, conv_w, conv_w, conv_b.reshape(1, -1), conv_b.reshape(1, -1), w_side)


def _stash_tile(t, res_ref, xo_ref, r_ref, ss_ref, *, bn):
    j = pl.program_id(1)
    xo_ref[:, pl.ds(pl.multiple_of(j * bn, bn), bn)] = t
    r_ref[j] = res_ref[...]
    ss_prev = jnp.where(j == 0, 0.0, ss_ref[...])
    ss_ref[...] = ss_prev + jnp.sum(t * t, axis=-1, keepdims=True)


def _finish_rows(r_ref, ss_ref, gp_ref, gn_ref, xo_ref, ho_ref, *, bn):
    nj = r_ref.shape[0]
    inv_d = 1.0 / (nj * bn)
    scale = lax.rsqrt(ss_ref[...] * inv_d + RMS_EPS)
    ss_new = jnp.zeros_like(scale)
    for jj in range(nj):
        cols = slice(jj * bn, (jj + 1) * bn)
        xn = r_ref[jj] + xo_ref[:, cols] * scale * gp_ref[:, cols]
        xo_ref[:, cols] = xn
        ss_new = ss_new + jnp.sum(xn * xn, axis=-1, keepdims=True)
    if ho_ref is not None:
        scale_next = lax.rsqrt(ss_new * inv_d + RMS_EPS)
        for jj in range(nj):
            cols = slice(jj * bn, (jj + 1) * bn)
            ho_ref[:, cols] = (xo_ref[:, cols] * scale_next * gn_ref[:, cols]).astype(ho_ref.dtype)


def _mm_resnorm_kernel(a_ref, w_ref, res_ref, gp_ref, gn_ref, xo_ref, ho_ref,
                       r_ref, ss_ref, *, bn):
    t = jnp.dot(a_ref[...], w_ref[...], preferred_element_type=F32)
    _stash_tile(t, res_ref, xo_ref, r_ref, ss_ref, bn=bn)

    @pl.when(pl.program_id(1) == pl.num_programs(1) - 1)
    def _():
        _finish_rows(r_ref, ss_ref, gp_ref, gn_ref, xo_ref, ho_ref, bn=bn)


def matmul_residual_norm(a, w, resid, g_post, g_next, *, bm, bn):
    s, k = a.shape
    d = w.shape[1]
    nj = d // bn
    assert s % bm == 0 and d % bn == 0
    row = pl.BlockSpec((bm, d), lambda i, j: (i, 0))
    vec = pl.BlockSpec((1, d), lambda i, j: (0, 0))
    return pl.pallas_call(
        functools.partial(_mm_resnorm_kernel, bn=bn),
        grid=(s // bm, nj),
        in_specs=[pl.BlockSpec((bm, k), lambda i, j: (i, 0)),
                  pl.BlockSpec((k, bn), lambda i, j: (0, j)),
                  pl.BlockSpec((bm, bn), lambda i, j: (i, j)),
                  vec, vec],
        out_specs=[row, row],
        out_shape=[jax.ShapeDtypeStruct((s, d), F32), jax.ShapeDtypeStruct((s, d), BF16)],
        scratch_shapes=[pltpu.VMEM((nj, bm, bn), F32), pltpu.VMEM((bm, 1), F32)],
        compiler_params=_params(2),
        name="proj_out_norm",
    )(a, w, resid, g_post.reshape(1, d), g_next.reshape(1, d))


def _ple_resnorm_kernel(h_ref, p_ref, wg_ref, wp_ref, x_ref, tp_ref, sp_ref, gpp_ref, gp_ref,
                        xo_ref, t_ref, ss_ref, *, ni, nj, bn):
    i, j = pl.program_id(0), pl.program_id(1)
    cur, done = 0, 1

    @pl.when(i > 0)
    def _():
        resid = x_ref[...] + tp_ref[...] * sp_ref[...] * gpp_ref[...]
        scale = lax.rsqrt(ss_ref[done] * (1.0 / (nj * bn)) + RMS_EPS)
        xo_ref[...] = resid + t_ref[j] * scale * gp_ref[...]

    @pl.when(i < ni)
    def _():
        g = jnp.dot(h_ref[...], wg_ref[...], preferred_element_type=F32)
        e = jnp.dot(p_ref[...].astype(BF16), wp_ref[...], preferred_element_type=F32)
        t = jax.nn.sigmoid(g) * e
        t_ref[j] = t
        ss = jnp.where(j == 0, 0.0, ss_ref[cur]) + jnp.sum(t * t, axis=-1, keepdims=True)
        ss_ref[cur] = ss

        @pl.when(j == nj - 1)
        def _():
            ss_ref[done] = ss


def ple_residual_norm(h, p, w_gate, w_ple, x, t_prev, scale_prev, g_prev, g_post, *, bm, bn):
    s, d = h.shape
    pd = p.shape[1]
    ni, nj = s // bm, d // bn
    assert s % bm == 0 and d % bn == 0
    mm_row = lambda i, j: (jnp.minimum(i, ni - 1), 0)
    out_tile = lambda i, j: (jnp.maximum(i - 1, 0), jnp.where(i == 0, 0, j))
    return pl.pallas_call(
        functools.partial(_ple_resnorm_kernel, ni=ni, nj=nj, bn=bn),
        grid=(ni + 1, nj),
        in_specs=[
            pl.BlockSpec((bm, d), mm_row),
            pl.BlockSpec((bm, pd), mm_row),
            pl.BlockSpec((d, bn), lambda i, j: (0, j)),
            pl.BlockSpec((pd, bn), lambda i, j: (0, j)),
            pl.BlockSpec((bm, bn), out_tile),
            pl.BlockSpec((bm, bn), out_tile),
            pl.BlockSpec((bm, 1), lambda i, j: (jnp.maximum(i - 1, 0), 0)),
            pl.BlockSpec((1, bn), lambda i, j: (0, j)),
            pl.BlockSpec((1, bn), lambda i, j: (0, j)),
        ],
        out_specs=pl.BlockSpec((bm, bn), out_tile),
        out_shape=jax.ShapeDtypeStruct((s, d), F32),
        scratch_shapes=[pltpu.VMEM((nj, bm, bn), F32), pltpu.VMEM((2, bm, 1), F32)],
        compiler_params=_params(2),
        name="ple_norm",
    )(h, p, w_gate, w_ple, x, t_prev, scale_prev, g_prev.reshape(1, d), g_post.reshape(1, d))


def _layer(x, p, norm_mix_pre, w_in, attn_sinks, w_pool, pool_scale, w_branch_attn,
           w_branch_pool, w_out, norm_mix_post, norm_ffn_pre, w_up, conv_w, conv_b,
           w_down, norm_ffn_post, norm_ple_gate, w_ple_gate, w_ple, norm_ple_post):
    d = x.shape[1]
    attn_width = w_branch_attn.shape[0]
    pool_width = w_branch_pool.shape[0]
    qkv_width = attn_width + 2 * N_KV_HEADS * HEAD_DIM
    h = rmsnorm_cast(x, norm_mix_pre)
    qkv = matmul_wcast(h, w_in, n_out=qkv_width, col_off=0, bm=1024, bn=512,
                       out_dtype=BF16, name="proj_qkv")
    u, w_ba, w_bp = matmul_wcast(h, w_in, n_out=pool_width, col_off=qkv_width, bm=1024, bn=512,
                                 out_dtype=F32, name="proj_pool_in",
                                 side_weights=(w_branch_attn, w_branch_pool))
    attn, w_gates = sliding_window_attention(qkv, attn_sinks, w_in, 2 * d)
    pool = multiscale_pool(u, w_pool.astype(BF16), pool_scale)
    merged, w_out_b, w_ple_gate_b = gated_merge(h, attn, pool, w_gates, w_ba, w_bp,
                                                (w_out, w_ple_gate), bm=1024, bn=256)
    x, h = matmul_residual_norm(merged, w_out_b, x, norm_mix_post, norm_ffn_pre,
                                bm=512, bn=512)

    act, w_down_b = conv_geglu_up(h, w_up, conv_w, conv_b, w_down, bm=2048, bn=256, n_sub=4)
    t = matmul(act, w_down_b, n_out=d, col_off=0, bm=512, bn=512,
               out_dtype=F32, name="proj_down")
    h, t_scale = residual_norm(x, t, norm_ffn_post, norm_ple_gate)

    return ple_residual_norm(h, p, w_ple_gate_b, w_ple.astype(BF16), x, t, t_scale,
                             norm_ffn_post, norm_ple_post, bm=1024, bn=512)


def kernel(x, p, norm_mix_pre, w_in, attn_sinks, w_pool, pool_scale, w_branch_attn,
           w_branch_pool, w_out, norm_mix_post, norm_ffn_pre, w_up, conv_w, conv_b,
           w_down, norm_ffn_post, norm_ple_gate, w_ple_gate, w_ple, norm_ple_post):
    batch, seq, d = x.shape
    depth = w_in.shape[0]
    assert batch == 1
    xs = x.reshape(seq, d)
    for i in range(depth):
        xs = _layer(xs, p[i, 0], norm_mix_pre[i], w_in[i], attn_sinks[i], w_pool[i],
                    pool_scale[i], w_branch_attn[i], w_branch_pool[i], w_out[i],
                    norm_mix_post[i], norm_ffn_pre[i], w_up[i], conv_w[i], conv_b[i],
                    w_down[i], norm_ffn_post[i], norm_ple_gate[i], w_ple_gate[i],
                    w_ple[i], norm_ple_post[i])
    return xs.reshape(batch, seq, d)
```

```python
import functools

import jax
import jax.numpy as jnp
from jax import lax
from jax.experimental import pallas as pl
from jax.experimental.pallas import tpu as pltpu

F32 = jnp.float32
BF16 = jnp.bfloat16

HEAD_DIM = 64
GQA_GROUP = 8
N_KV_HEADS = 4
ATTN_BLOCK = 128
ATTN_Q_BLOCKS = 4
POOL_WINDOWS = (2, 4, 8, 16)
POOL_HALO = 16
CONV_WIDTH = 3
RMS_EPS = 1e-6
MASK_VALUE = -1e30

LANES = 128
SUBLANES = 8
VMEM_LIMIT_BYTES = 60 * 1024 * 1024
CAST_ROWS = 512


def _params(n_grid_dims):
    return pltpu.CompilerParams(
        dimension_semantics=("arbitrary",) * n_grid_dims,
        vmem_limit_bytes=VMEM_LIMIT_BYTES)


def _rms(x, gain):
    ms = jnp.mean(x * x, axis=-1, keepdims=True)
    return x * lax.rsqrt(ms + RMS_EPS) * gain


def _norm_kernel(x_ref, g_ref, o_ref):
    o_ref[...] = _rms(x_ref[...], g_ref[...]).astype(o_ref.dtype)


def rmsnorm_cast(x, gain, *, bm=512):
    s, d = x.shape
    return pl.pallas_call(
        _norm_kernel,
        grid=(s // bm,),
        in_specs=[pl.BlockSpec((bm, d), lambda i: (i, 0)),
                  pl.BlockSpec((1, d), lambda i: (0, 0))],
        out_specs=pl.BlockSpec((bm, d), lambda i: (i, 0)),
        out_shape=jax.ShapeDtypeStruct((s, d), BF16),
        compiler_params=_params(1),
        name="rmsnorm_cast",
    )(x, gain.reshape(1, d))


def _resnorm_kernel(x_ref, t_ref, gp_ref, gn_ref, ho_ref, sc_ref):
    t = t_ref[...]
    scale = lax.rsqrt(jnp.mean(t * t, axis=-1, keepdims=True) + RMS_EPS)
    sc_ref[...] = scale
    xn = x_ref[...] + t * scale * gp_ref[...]
    ho_ref[...] = _rms(xn, gn_ref[...]).astype(ho_ref.dtype)


def residual_norm(x, t, g_post, g_next, *, bm=256):
    s, d = x.shape
    row = pl.BlockSpec((bm, d), lambda i: (i, 0))
    vec = pl.BlockSpec((1, d), lambda i: (0, 0))
    return pl.pallas_call(
        _resnorm_kernel, grid=(s // bm,),
        in_specs=[row, row, vec, vec],
        out_specs=[row, pl.BlockSpec((bm, 1), lambda i: (i, 0))],
        out_shape=[jax.ShapeDtypeStruct((s, d), BF16), jax.ShapeDtypeStruct((s, 1), F32)],
        compiler_params=_params(1), name="residual_norm",
    )(x, t, g_post.reshape(1, d), g_next.reshape(1, d))


def _mm_kernel(x_ref, w_ref, o_ref):
    o_ref[...] = jnp.dot(x_ref[...], w_ref[...],
                         preferred_element_type=F32).astype(o_ref.dtype)


def _side_cast_specs(src, n_steps, step_of, n_cols=None):
    k, width = src.shape
    n_cols = width if n_cols is None else n_cols
    rows = k // n_steps
    assert k % n_steps == 0 and rows % (2 * SUBLANES) == 0
    src_index = lambda *g: (step_of(*g) * rows, width - n_cols)
    dst_index = lambda *g: (step_of(*g), 0)
    return (pl.BlockSpec((pl.Element(rows), pl.Element(n_cols)), src_index),
            pl.BlockSpec((rows, n_cols), dst_index),
            jax.ShapeDtypeStruct((k, n_cols), BF16))


def _side_cast(src_ref, dst_ref):
    dst_ref[...] = src_ref[...].astype(BF16)


def _stage_bf16(dst_ref, col0, src_ref):
    k, n = src_ref.shape

    def cast_rows(c, carry):
        rows = pl.ds(pl.multiple_of(c * CAST_ROWS, CAST_ROWS), CAST_ROWS)
        dst_ref[rows, col0:col0 + n] = src_ref[rows, :].astype(BF16)
        return carry
    lax.fori_loop(0, k // CAST_ROWS, cast_rows, 0)


def _mm_wcast_kernel(x_ref, w_ref, *rest, n_side):
    side_src, o_ref = rest[:n_side], rest[n_side]
    side_dst, wb_ref = rest[n_side + 1:2 * n_side + 1], rest[2 * n_side + 1]

    @pl.when(pl.program_id(1) == 0)
    def _():
        _stage_bf16(wb_ref, 0, w_ref)
    o_ref[...] = jnp.dot(x_ref[...], wb_ref[...],
                         preferred_element_type=F32).astype(o_ref.dtype)
    for src_ref, dst_ref in zip(side_src, side_dst):
        _side_cast(src_ref, dst_ref)


def matmul_wcast(x, w, *, n_out, col_off, bm, bn, out_dtype, name, side_weights=()):
    m, k = x.shape
    off = col_off // bn
    assert col_off % bn == 0 and n_out % bn == 0 and m % bm == 0 and k % CAST_ROWS == 0
    ni = m // bm
    side = [_side_cast_specs(sw, (n_out // bn) * ni, lambda j, i: j * ni + i)
            for sw in side_weights]
    outs = pl.pallas_call(
        functools.partial(_mm_wcast_kernel, n_side=len(side)),
        grid=(n_out // bn, ni),
        in_specs=[pl.BlockSpec((bm, k), lambda j, i: (i, 0)),
                  pl.BlockSpec((k, bn), lambda j, i: (0, j + off))] + [sp[0] for sp in side],
        out_specs=[pl.BlockSpec((bm, bn), lambda j, i: (i, j))] + [sp[1] for sp in side],
        out_shape=[jax.ShapeDtypeStruct((m, n_out), out_dtype)] + [sp[2] for sp in side],
        scratch_shapes=[pltpu.VMEM((k, bn), BF16)],
        compiler_params=_params(2),
        name=name,
    )(x, w, *side_weights)
    return outs if side else outs[0]


def matmul(x, w, *, n_out, col_off, bm, bn, out_dtype, name):
    m, k = x.shape
    off = col_off // bn
    assert col_off % bn == 0 and n_out % bn == 0 and m % bm == 0
    return pl.pallas_call(
        _mm_kernel,
        grid=(m // bm, n_out // bn),
        in_specs=[pl.BlockSpec((bm, k), lambda i, j: (i, 0)),
                  pl.BlockSpec((k, bn), lambda i, j: (0, j + off))],
        out_specs=pl.BlockSpec((bm, bn), lambda i, j: (i, j)),
        out_shape=jax.ShapeDtypeStruct((m, n_out), out_dtype),
        compiler_params=_params(2),
        name=name,
    )(x, w)


def _attn_kernel(sink_ref, q_ref, kp_ref, kc_ref, vp_ref, vc_ref, bias_ref, wsrc_ref,
                 o_ref, wdst_ref):
    _side_cast(wsrc_ref, wdst_ref)
    n, hp = pl.program_id(0), pl.program_id(1)
    nkeys = 2 * ATTN_BLOCK
    lo = lax.broadcasted_iota(jnp.int32, (nkeys, LANES), 1) < HEAD_DIM
    lo_q = lax.broadcasted_iota(jnp.int32, (ATTN_BLOCK, LANES), 1) < HEAD_DIM
    k_all = jnp.concatenate([kp_ref[...], kc_ref[...]], axis=0).astype(F32)
    v_all = jnp.concatenate([vp_ref[...], vc_ref[...]], axis=0).astype(F32)
    k_all_r = pltpu.roll(k_all, HEAD_DIM, axis=1)
    v_all_r = pltpu.roll(v_all, HEAD_DIM, axis=1)
    zero = jnp.zeros((nkeys, LANES), F32)
    rows_per_kv = GQA_GROUP // 2
    for qb in range(ATTN_Q_BLOCKS):
        qrows = slice(qb * ATTN_BLOCK, (qb + 1) * ATTN_BLOCK)
        krows = slice(qb * ATTN_BLOCK, qb * ATTN_BLOCK + nkeys)
        k2, k2r, v2, v2r = k_all[krows], k_all_r[krows], v_all[krows], v_all_r[krows]
        bias = bias_ref[jnp.minimum(n, 1)] if qb == 0 else bias_ref[1]
        for hh in range(2):
            kd = jnp.where(lo, k2, k2r) if hh == 0 else jnp.where(lo, k2r, k2)
            vd = jnp.where(lo, v2, v2r) if hh == 0 else jnp.where(lo, v2r, v2)
            kbd = jnp.concatenate([jnp.where(lo, kd, zero), jnp.where(lo, zero, kd)],
                                  axis=0).astype(BF16)
            vbd = jnp.concatenate([jnp.where(lo, vd, zero), jnp.where(lo, zero, vd)],
                                  axis=0).astype(BF16)
            base = hh * rows_per_kv * LANES
            qs = jnp.concatenate(
                [q_ref[qrows, base + r * LANES: base + (r + 1) * LANES]
                 for r in range(rows_per_kv)], axis=0)
            s = lax.dot_general(qs * jnp.asarray(HEAD_DIM ** -0.5, BF16), kbd,
                                (((1,), (1,)), ((), ())),
                                preferred_element_type=F32) + bias
            p_rows, inv_rows = [], []
            for r in range(rows_per_kv):
                halves, invs = [], []
                for c in range(2):
                    sink = sink_ref[(2 * hp + hh) * GQA_GROUP + 2 * r + c]
                    blk = s[r * ATTN_BLOCK:(r + 1) * ATTN_BLOCK, c * nkeys:(c + 1) * nkeys]
                    m = jnp.maximum(jnp.max(blk, axis=-1, keepdims=True), sink)
                    e = jnp.exp(blk - m)
                    denom = jnp.sum(e, axis=-1, keepdims=True) + jnp.exp(sink - m)
                    halves.append(e.astype(BF16))
                    invs.append(1.0 / denom)
                p_rows.append(jnp.concatenate(halves, axis=1))
                inv_rows.append(jnp.where(lo_q, invs[0], invs[1]))
            p = jnp.concatenate(p_rows, axis=0)
            o = jnp.dot(p, vbd, preferred_element_type=F32)
            for r in range(rows_per_kv):
                o_ref[qrows, base + r * LANES: base + (r + 1) * LANES] = (
                    o[r * ATTN_BLOCK:(r + 1) * ATTN_BLOCK] * inv_rows[r]).astype(o_ref.dtype)


def _attn_bias():
    nkeys = 2 * ATTN_BLOCK
    rows = (GQA_GROUP // 2) * ATTN_BLOCK
    qi = (jnp.arange(rows) % ATTN_BLOCK)[:, None]
    kj = (jnp.arange(2 * nkeys) % nkeys)[None, :]
    rel = kj - ATTN_BLOCK - qi
    in_band = (rel <= 0) & (rel > -ATTN_BLOCK)
    first = in_band & (kj >= ATTN_BLOCK)
    return jnp.where(jnp.stack([first, in_band]), 0.0, MASK_VALUE).astype(F32)


def sliding_window_attention(qkv, sinks, w_side, n_side_cols):
    s = qkv.shape[0]
    n_hp = N_KV_HEADS // 2
    q_rows = ATTN_Q_BLOCKS * ATTN_BLOCK
    assert s % q_rows == 0
    side_in, side_out, side_shape = _side_cast_specs(
        w_side, (s // q_rows) * n_hp, lambda n, hp: n * n_hp + hp, n_side_cols)
    q_width = N_KV_HEADS * GQA_GROUP * HEAD_DIM
    pair = 2 * GQA_GROUP * HEAD_DIM
    k_blk = q_width // LANES
    v_blk = k_blk + N_KV_HEADS * HEAD_DIM // LANES
    rows = (GQA_GROUP // 2) * ATTN_BLOCK
    prev = lambda n: jnp.maximum(n * ATTN_Q_BLOCKS - 1, 0)
    kv_prev, kv_cur = (ATTN_BLOCK, LANES), (q_rows, LANES)
    return pl.pallas_call(
        _attn_kernel,
        grid=(s // q_rows, n_hp),
        in_specs=[
            pl.BlockSpec(memory_space=pltpu.SMEM),
            pl.BlockSpec((q_rows, pair), lambda n, hp: (n, hp)),
            pl.BlockSpec(kv_prev, lambda n, hp: (prev(n), k_blk + hp)),
            pl.BlockSpec(kv_cur, lambda n, hp: (n, k_blk + hp)),
            pl.BlockSpec(kv_prev, lambda n, hp: (prev(n), v_blk + hp)),
            pl.BlockSpec(kv_cur, lambda n, hp: (n, v_blk + hp)),
            pl.BlockSpec((2, rows, 4 * ATTN_BLOCK), lambda n, hp: (0, 0, 0)),
            side_in,
        ],
        out_specs=[pl.BlockSpec((q_rows, pair), lambda n, hp: (n, hp)), side_out],
        out_shape=[jax.ShapeDtypeStruct((s, q_width), BF16), side_shape],
        compiler_params=_params(2),
        name="swa_attention",
    )(sinks, qkv, qkv, qkv, qkv, qkv, _attn_bias(), w_side)


def _pool_kernel(u_ref, halo_ref, wp_ref, sc_ref, o_ref, *, bm, gdim):
    i = pl.program_id(0)
    t = lax.broadcasted_iota(jnp.int32, (bm, 1), 0) + i * bm
    for g, w in enumerate(POOL_WINDOWS):
        cols = slice(g * gdim, (g + 1) * gdim)
        u = u_ref[:, cols]
        halo = jnp.where(i > 0, halo_ref[:, cols], 0.0)
        acc = jnp.concatenate([halo, u], axis=0)
        k = 1
        while k < w:
            acc = acc + pltpu.roll(acc, k, axis=0)
            k *= 2
        count = jnp.minimum(t + 1, w).astype(F32)
        mixed = (acc[POOL_HALO:] * (1.0 / count) - u).astype(BF16)
        y = jnp.dot(mixed, wp_ref[g], preferred_element_type=F32) * sc_ref[:, cols]
        o_ref[:, cols] = y.astype(o_ref.dtype)


def multiscale_pool(u, w_pool, pool_scale, *, bm=512):
    s, width = u.shape
    ngroups, gdim, _ = w_pool.shape
    return pl.pallas_call(
        functools.partial(_pool_kernel, bm=bm, gdim=gdim),
        grid=(s // bm,),
        in_specs=[
            pl.BlockSpec((bm, width), lambda i: (i, 0)),
            pl.BlockSpec((POOL_HALO, width),
                         lambda i: (jnp.maximum(i * (bm // POOL_HALO) - 1, 0), 0)),
            pl.BlockSpec((ngroups, gdim, gdim), lambda i: (0, 0, 0)),
            pl.BlockSpec((1, width), lambda i: (0, 0)),
        ],
        out_specs=pl.BlockSpec((bm, width), lambda i: (i, 0)),
        out_shape=jax.ShapeDtypeStruct((s, width), BF16),
        compiler_params=_params(1),
        name="multiscale_pool",
    )(u, u, w_pool, pool_scale.reshape(1, width))


def _merge_kernel(h_ref, a_ref, p_ref, wga_ref, wgp_ref, wba_ref, wbp_ref, s0_ref, s1_ref,
                  o_ref, d0_ref, d1_ref):
    h = h_ref[...]
    dot = functools.partial(jnp.dot, preferred_element_type=F32)
    ya = jax.nn.sigmoid(dot(h, wga_ref[...])) * dot(a_ref[...], wba_ref[...])
    yp = jax.nn.sigmoid(dot(h, wgp_ref[...])) * dot(p_ref[...], wbp_ref[...])
    o_ref[...] = (ya + yp).astype(o_ref.dtype)
    _side_cast(s0_ref, d0_ref)
    _side_cast(s1_ref, d1_ref)


def gated_merge(h, attn, pool, w_gates, w_ba, w_bp, side_weights, *, bm, bn):
    s, d = h.shape
    ka, kp = attn.shape[1], pool.shape[1]
    nj = d // bn
    assert d % bn == 0 and s % bm == 0
    side = [_side_cast_specs(sw, (s // bm) * nj, lambda i, j: i * nj + j) for sw in side_weights]
    return pl.pallas_call(
        _merge_kernel,
        grid=(s // bm, nj),
        in_specs=[
            pl.BlockSpec((bm, d), lambda i, j: (i, 0)),
            pl.BlockSpec((bm, ka), lambda i, j: (i, 0)),
            pl.BlockSpec((bm, kp), lambda i, j: (i, 0)),
            pl.BlockSpec((d, bn), lambda i, j: (0, j)),
            pl.BlockSpec((d, bn), lambda i, j: (0, j + nj)),
            pl.BlockSpec((ka, bn), lambda i, j: (0, j)),
            pl.BlockSpec((kp, bn), lambda i, j: (0, j)),
            side[0][0], side[1][0],
        ],
        out_specs=[pl.BlockSpec((bm, bn), lambda i, j: (i, j)), side[0][1], side[1][1]],
        out_shape=[jax.ShapeDtypeStruct((s, d), BF16), side[0][2], side[1][2]],
        compiler_params=_params(2),
        name="gated_merge",
    )(h, attn, pool, w_gates, w_gates, w_ba, w_bp, *side_weights)


def _ffn_up_kernel(h_ref, wg_ref, wv_ref, cwg_ref, cwv_ref, cbg_ref, cbv_ref, wsrc_ref,
                   o_ref, wdst_ref, w_ref, u_ref, *, bm, bn, n_sub):
    i = pl.program_id(1)
    _side_cast(wsrc_ref, wdst_ref)

    @pl.when(i == 0)
    def _():
        _stage_bf16(w_ref, 0, wg_ref)
        _stage_bf16(w_ref, bn, wv_ref)
        u_ref[...] = jnp.zeros(u_ref.shape, F32)

    cw = jnp.concatenate([cwg_ref[...], cwv_ref[...]], axis=1)
    cb = jnp.concatenate([cbg_ref[...], cbv_ref[...]], axis=1)
    sub = bm // n_sub
    prev = u_ref[...]
    ups = [jnp.dot(h_ref[r * sub:(r + 1) * sub, :], w_ref[...], preferred_element_type=F32)
           for r in range(n_sub)]
    for r in range(n_sub):
        lo = r * sub
        up = ups[r]
        ext = jnp.concatenate([prev, up], axis=0)
        y = cb + up * cw[CONV_WIDTH - 1:CONV_WIDTH, :]
        for back in range(1, CONV_WIDTH):
            y = y + (pltpu.roll(ext, back, axis=0)[SUBLANES:]
                     * cw[CONV_WIDTH - 1 - back:CONV_WIDTH - back, :])
        o_ref[lo:lo + sub, :] = (jax.nn.gelu(y[:, :bn], approximate=True)
                                 * y[:, bn:]).astype(o_ref.dtype)
        prev = up[sub - SUBLANES:]
    u_ref[...] = prev


def conv_geglu_up(h, w_up, conv_w, conv_b, w_side, *, bm, bn, n_sub):
    s, d = h.shape
    d_ff = w_up.shape[1] // 2
    nj = d_ff // bn
    ni = s // bm
    assert d_ff % bn == 0 and d % CAST_ROWS == 0 and bm % n_sub == 0
    side_in, side_out, side_shape = _side_cast_specs(w_side, nj * ni, lambda j, i: j * ni + i)
    return pl.pallas_call(
        functools.partial(_ffn_up_kernel, bm=bm, bn=bn, n_sub=n_sub),
        grid=(nj, s // bm),
        in_specs=[
            pl.BlockSpec((bm, d), lambda j, i: (i, 0)),
            pl.BlockSpec((d, bn), lambda j, i: (0, j)),
            pl.BlockSpec((d, bn), lambda j, i: (0, j + nj)),
            pl.BlockSpec((CONV_WIDTH, bn), lambda j, i: (0, j)),
            pl.BlockSpec((CONV_WIDTH, bn), lambda j, i: (0, j + nj)),
            pl.BlockSpec((1, bn), lambda j, i: (0, j)),
            pl.BlockSpec((1, bn), lambda j, i: (0, j + nj)),
            side_in,
        ],
        out_specs=[pl.BlockSpec((bm, bn), lambda j, i: (i, j)), side_out],
        out_shape=[jax.ShapeDtypeStruct((s, d_ff), BF16), side_shape],
        scratch_shapes=[pltpu.VMEM((d, 2 * bn), BF16),
                        pltpu.VMEM((SUBLANES, 2 * bn), F32)],
        compiler_params=_params(2),
        name="conv_geglu_up",
    )(h, w_up, w_up, conv_w, conv_w, conv_b.reshape(1, -1), conv_b.reshape(1, -1), w_side)


def _stash_tile(t, res_ref, xo_ref, r_ref, ss_ref, *, bn):
    j = pl.program_id(1)
    xo_ref[:, pl.ds(pl.multiple_of(j * bn, bn), bn)] = t
    r_ref[j] = res_ref[...]
    ss_prev = jnp.where(j == 0, 0.0, ss_ref[...])
    ss_ref[...] = ss_prev + jnp.sum(t * t, axis=-1, keepdims=True)


def _finish_rows(r_ref, ss_ref, gp_ref, gn_ref, xo_ref, ho_ref, *, bn):
    nj = r_ref.shape[0]
    inv_d = 1.0 / (nj * bn)
    scale = lax.rsqrt(ss_ref[...] * inv_d + RMS_EPS)
    ss_new = jnp.zeros_like(scale)
    for jj in range(nj):
        cols = slice(jj * bn, (jj + 1) * bn)
        xn = r_ref[jj] + xo_ref[:, cols] * scale * gp_ref[:, cols]
        xo_ref[:, cols] = xn
        ss_new = ss_new + jnp.sum(xn * xn, axis=-1, keepdims=True)
    if ho_ref is not None:
        scale_next = lax.rsqrt(ss_new * inv_d + RMS_EPS)
        for jj in range(nj):
            cols = slice(jj * bn, (jj + 1) * bn)
            ho_ref[:, cols] = (xo_ref[:, cols] * scale_next * gn_ref[:, cols]).astype(ho_ref.dtype)


def _mm_resnorm_kernel(a_ref, w_ref, res_ref, gp_ref, gn_ref, xo_ref, ho_ref,
                       r_ref, ss_ref, *, bn):
    t = jnp.dot(a_ref[...], w_ref[...], preferred_element_type=F32)
    _stash_tile(t, res_ref, xo_ref, r_ref, ss_ref, bn=bn)

    @pl.when(pl.program_id(1) == pl.num_programs(1) - 1)
    def _():
        _finish_rows(r_ref, ss_ref, gp_ref, gn_ref, xo_ref, ho_ref, bn=bn)


def matmul_residual_norm(a, w, resid, g_post, g_next, *, bm, bn):
    s, k = a.shape
    d = w.shape[1]
    nj = d // bn
    assert s % bm == 0 and d % bn == 0
    row = pl.BlockSpec((bm, d), lambda i, j: (i, 0))
    vec = pl.BlockSpec((1, d), lambda i, j: (0, 0))
    return pl.pallas_call(
        functools.partial(_mm_resnorm_kernel, bn=bn),
        grid=(s // bm, nj),
        in_specs=[pl.BlockSpec((bm, k), lambda i, j: (i, 0)),
                  pl.BlockSpec((k, bn), lambda i, j: (0, j)),
                  pl.BlockSpec((bm, bn), lambda i, j: (i, j)),
                  vec, vec],
        out_specs=[row, row],
        out_shape=[jax.ShapeDtypeStruct((s, d), F32), jax.ShapeDtypeStruct((s, d), BF16)],
        scratch_shapes=[pltpu.VMEM((nj, bm, bn), F32), pltpu.VMEM((bm, 1), F32)],
        compiler_params=_params(2),
        name="proj_out_norm",
    )(a, w, resid, g_post.reshape(1, d), g_next.reshape(1, d))


def _ple_resnorm_kernel(h_ref, p_ref, wg_ref, wp_ref, x_ref, tp_ref, sp_ref, gpp_ref, gp_ref,
                        xo_ref, t_ref, ss_ref, *, ni, nj, bn):
    i, j = pl.program_id(0), pl.program_id(1)
    cur, done = 0, 1

    @pl.when(i > 0)
    def _():
        resid = x_ref[...] + tp_ref[...] * sp_ref[...] * gpp_ref[...]
        scale = lax.rsqrt(ss_ref[done] * (1.0 / (nj * bn)) + RMS_EPS)
        xo_ref[...] = resid + t_ref[j] * scale * gp_ref[...]

    @pl.when(i < ni)
    def _():
        g = jnp.dot(h_ref[...], wg_ref[...], preferred_element_type=F32)
        e = jnp.dot(p_ref[...].astype(BF16), wp_ref[...], preferred_element_type=F32)
        t = jax.nn.sigmoid(g) * e
        t_ref[j] = t
        ss = jnp.where(j == 0, 0.0, ss_ref[cur]) + jnp.sum(t * t, axis=-1, keepdims=True)
        ss_ref[cur] = ss

        @pl.when(j == nj - 1)
        def _():
            ss_ref[done] = ss


def ple_residual_norm(h, p, w_gate, w_ple, x, t_prev, scale_prev, g_prev, g_post, *, bm, bn):
    s, d = h.shape
    pd = p.shape[1]
    ni, nj = s // bm, d // bn
    assert s % bm == 0 and d % bn == 0
    mm_row = lambda i, j: (jnp.minimum(i, ni - 1), 0)
    out_tile = lambda i, j: (jnp.maximum(i - 1, 0), jnp.where(i == 0, 0, j))
    return pl.pallas_call(
        functools.partial(_ple_resnorm_kernel, ni=ni, nj=nj, bn=bn),
        grid=(ni + 1, nj),
        in_specs=[
            pl.BlockSpec((bm, d), mm_row),
            pl.BlockSpec((bm, pd), mm_row),
            pl.BlockSpec((d, bn), lambda i, j: (0, j)),
            pl.BlockSpec((pd, bn), lambda i, j: (0, j)),
            pl.BlockSpec((bm, bn), out_tile),
            pl.BlockSpec((bm, bn), out_tile),
            pl.BlockSpec((bm, 1), lambda i, j: (jnp.maximum(i - 1, 0), 0)),
            pl.BlockSpec((1, bn), lambda i, j: (0, j)),
            pl.BlockSpec((1, bn), lambda i, j: (0, j)),
        ],
        out_specs=pl.BlockSpec((bm, bn), out_tile),
        out_shape=jax.ShapeDtypeStruct((s, d), F32),
        scratch_shapes=[pltpu.VMEM((nj, bm, bn), F32), pltpu.VMEM((2, bm, 1), F32)],
        compiler_params=_params(2),
        name="ple_norm",
    )(h, p, w_gate, w_ple, x, t_prev, scale_prev, g_prev.reshape(1, d), g_post.reshape(1, d))


def _layer(x, p, norm_mix_pre, w_in, attn_sinks, w_pool, pool_scale, w_branch_attn,
           w_branch_pool, w_out, norm_mix_post, norm_ffn_pre, w_up, conv_w, conv_b,
           w_down, norm_ffn_post, norm_ple_gate, w_ple_gate, w_ple, norm_ple_post):
    d = x.shape[1]
    attn_width = w_branch_attn.shape[0]
    pool_width = w_branch_pool.shape[0]
    qkv_width = attn_width + 2 * N_KV_HEADS * HEAD_DIM
    h = rmsnorm_cast(x, norm_mix_pre)
    qkv = matmul_wcast(h, w_in, n_out=qkv_width, col_off=0, bm=1024, bn=512,
                       out_dtype=BF16, name="proj_qkv")
    u, w_ba, w_bp = matmul_wcast(h, w_in, n_out=pool_width, col_off=qkv_width, bm=1024, bn=512,
                                 out_dtype=F32, name="proj_pool_in",
                                 side_weights=(w_branch_attn, w_branch_pool))
    attn, w_gates = sliding_window_attention(qkv, attn_sinks, w_in, 2 * d)
    pool = multiscale_pool(u, w_pool.astype(BF16), pool_scale)
    merged, w_out_b, w_ple_gate_b = gated_merge(h, attn, pool, w_gates, w_ba, w_bp,
                                                (w_out, w_ple_gate), bm=1024, bn=256)
    x, h = matmul_residual_norm(merged, w_out_b, x, norm_mix_post, norm_ffn_pre,
                                bm=512, bn=512)

    act, w_down_b = conv_geglu_up(h, w_up, conv_w, conv_b, w_down, bm=2048, bn=256, n_sub=4)
    t = matmul(act, w_down_b, n_out=d, col_off=0, bm=512, bn=512,
               out_dtype=F32, name="proj_down")
    h, t_scale = residual_norm(x, t, norm_ffn_post, norm_ple_gate)

    return ple_residual_norm(h, p, w_ple_gate_b, w_ple.astype(BF16), x, t, t_scale,
                             norm_ffn_post, norm_ple_post, bm=1024, bn=512)


def kernel(x, p, norm_mix_pre, w_in, attn_sinks, w_pool, pool_scale, w_branch_attn,
           w_branch_pool, w_out, norm_mix_post, norm_ffn_pre, w_up, conv_w, conv_b,
           w_down, norm_ffn_post, norm_ple_gate, w_ple_gate, w_ple, norm_ple_post):
    batch, seq, d = x.shape
    depth = w_in.shape[0]
    assert batch == 1
    xs = x.reshape(seq, d)
    for i in range(depth):
        xs = _layer(xs, p[i, 0], norm_mix_pre[i], w_in[i], attn_sinks[i], w_pool[i],
                    pool_scale[i], w_branch_attn[i], w_branch_pool[i], w_out[i],
                    norm_mix_post[i], norm_ffn_pre[i], w_up[i], conv_w[i], conv_b[i],
                    w_down[i], norm_ffn_post[i], norm_ple_gate[i], w_ple_gate[i],
                    w_ple[i], norm_ple_post[i])
    return xs.reshape(batch, seq, d)
```

```python
import functools

import jax
import jax.numpy as jnp
from jax import lax
from jax.experimental import pallas as pl
from jax.experimental.pallas import tpu as pltpu

F32 = jnp.float32
BF16 = jnp.bfloat16

HEAD_DIM = 64
GQA_GROUP = 8
N_KV_HEADS = 4
ATTN_BLOCK = 128
ATTN_Q_BLOCKS = 8
POOL_WINDOWS = (2, 4, 8, 16)
POOL_HALO = 16
CONV_WIDTH = 3
RMS_EPS = 1e-6
MASK_VALUE = -1e30

LANES = 128
SUBLANES = 8
VMEM_LIMIT_BYTES = 60 * 1024 * 1024
CAST_ROWS = 512


def _params(n_grid_dims):
    return pltpu.CompilerParams(
        dimension_semantics=("arbitrary",) * n_grid_dims,
        vmem_limit_bytes=VMEM_LIMIT_BYTES)


def _rms(x, gain):
    ms = jnp.mean(x * x, axis=-1, keepdims=True)
    return x * lax.rsqrt(ms + RMS_EPS) * gain


def _norm_kernel(x_ref, g_ref, o_ref):
    o_ref[...] = _rms(x_ref[...], g_ref[...]).astype(o_ref.dtype)


def rmsnorm_cast(x, gain, *, bm=512):
    s, d = x.shape
    return pl.pallas_call(
        _norm_kernel,
        grid=(s // bm,),
        in_specs=[pl.BlockSpec((bm, d), lambda i: (i, 0)),
                  pl.BlockSpec((1, d), lambda i: (0, 0))],
        out_specs=pl.BlockSpec((bm, d), lambda i: (i, 0)),
        out_shape=jax.ShapeDtypeStruct((s, d), BF16),
        compiler_params=_params(1),
        name="rmsnorm_cast",
    )(x, gain.reshape(1, d))


def _resnorm_kernel(x_ref, t_ref, gp_ref, gn_ref, ho_ref, sc_ref):
    t = t_ref[...]
    scale = lax.rsqrt(jnp.mean(t * t, axis=-1, keepdims=True) + RMS_EPS)
    sc_ref[...] = scale
    xn = x_ref[...] + t * scale * gp_ref[...]
    ho_ref[...] = _rms(xn, gn_ref[...]).astype(ho_ref.dtype)


def residual_norm(x, t, g_post, g_next, *, bm=512):
    s, d = x.shape
    row = pl.BlockSpec((bm, d), lambda i: (i, 0))
    vec = pl.BlockSpec((1, d), lambda i: (0, 0))
    return pl.pallas_call(
        _resnorm_kernel, grid=(s // bm,),
        in_specs=[row, row, vec, vec],
        out_specs=[row, pl.BlockSpec((bm, 1), lambda i: (i, 0))],
        out_shape=[jax.ShapeDtypeStruct((s, d), BF16), jax.ShapeDtypeStruct((s, 1), F32)],
        compiler_params=_params(1), name="residual_norm",
    )(x, t, g_post.reshape(1, d), g_next.reshape(1, d))


def _mm_kernel(x_ref, w_ref, o_ref):
    o_ref[...] = jnp.dot(x_ref[...], w_ref[...],
                         preferred_element_type=F32).astype(o_ref.dtype)


def _side_cast_specs(src, n_steps, step_of, n_cols=None):
    k, width = src.shape
    n_cols = width if n_cols is None else n_cols
    rows = k // n_steps
    assert k % n_steps == 0 and rows % (2 * SUBLANES) == 0
    src_index = lambda *g: (step_of(*g) * rows, width - n_cols)
    dst_index = lambda *g: (step_of(*g), 0)
    return (pl.BlockSpec((pl.Element(rows), pl.Element(n_cols)), src_index),
            pl.BlockSpec((rows, n_cols), dst_index),
            jax.ShapeDtypeStruct((k, n_cols), BF16))


def _side_cast(src_ref, dst_ref):
    dst_ref[...] = src_ref[...].astype(BF16)


def _stage_bf16(dst_ref, col0, src_ref):
    k, n = src_ref.shape

    def cast_rows(c, carry):
        rows = pl.ds(pl.multiple_of(c * CAST_ROWS, CAST_ROWS), CAST_ROWS)
        dst_ref[rows, col0:col0 + n] = src_ref[rows, :].astype(BF16)
        return carry
    lax.fori_loop(0, k // CAST_ROWS, cast_rows, 0)


def _mm_wcast_kernel(x_ref, w_ref, *rest, n_side):
    side_src, o_ref = rest[:n_side], rest[n_side]
    side_dst, wb_ref = rest[n_side + 1:2 * n_side + 1], rest[2 * n_side + 1]

    @pl.when(pl.program_id(1) == 0)
    def _():
        _stage_bf16(wb_ref, 0, w_ref)
    o_ref[...] = jnp.dot(x_ref[...], wb_ref[...],
                         preferred_element_type=F32).astype(o_ref.dtype)
    for src_ref, dst_ref in zip(side_src, side_dst):
        _side_cast(src_ref, dst_ref)


def matmul_wcast(x, w, *, n_out, col_off, bm, bn, out_dtype, name, side_weights=()):
    m, k = x.shape
    off = col_off // bn
    assert col_off % bn == 0 and n_out % bn == 0 and m % bm == 0 and k % CAST_ROWS == 0
    ni = m // bm
    side = [_side_cast_specs(sw, (n_out // bn) * ni, lambda j, i: j * ni + i)
            for sw in side_weights]
    outs = pl.pallas_call(
        functools.partial(_mm_wcast_kernel, n_side=len(side)),
        grid=(n_out // bn, ni),
        in_specs=[pl.BlockSpec((bm, k), lambda j, i: (i, 0)),
                  pl.BlockSpec((k, bn), lambda j, i: (0, j + off))] + [sp[0] for sp in side],
        out_specs=[pl.BlockSpec((bm, bn), lambda j, i: (i, j))] + [sp[1] for sp in side],
        out_shape=[jax.ShapeDtypeStruct((m, n_out), out_dtype)] + [sp[2] for sp in side],
        scratch_shapes=[pltpu.VMEM((k, bn), BF16)],
        compiler_params=_params(2),
        name=name,
    )(x, w, *side_weights)
    return outs if side else outs[0]


def matmul(x, w, *, n_out, col_off, bm, bn, out_dtype, name):
    m, k = x.shape
    off = col_off // bn
    assert col_off % bn == 0 and n_out % bn == 0 and m % bm == 0
    return pl.pallas_call(
        _mm_kernel,
        grid=(m // bm, n_out // bn),
        in_specs=[pl.BlockSpec((bm, k), lambda i, j: (i, 0)),
                  pl.BlockSpec((k, bn), lambda i, j: (0, j + off))],
        out_specs=pl.BlockSpec((bm, bn), lambda i, j: (i, j)),
        out_shape=jax.ShapeDtypeStruct((m, n_out), out_dtype),
        compiler_params=_params(2),
        name=name,
    )(x, w)


def _attn_kernel(sink_ref, q_ref, kp_ref, kc_ref, vp_ref, vc_ref, bias_ref, wsrc_ref,
                 o_ref, wdst_ref):
    _side_cast(wsrc_ref, wdst_ref)
    n, hp = pl.program_id(0), pl.program_id(1)
    nkeys = 2 * ATTN_BLOCK
    lo = lax.broadcasted_iota(jnp.int32, (nkeys, LANES), 1) < HEAD_DIM
    lo_q = lax.broadcasted_iota(jnp.int32, (ATTN_BLOCK, LANES), 1) < HEAD_DIM
    k_all = jnp.concatenate([kp_ref[...], kc_ref[...]], axis=0).astype(F32)
    v_all = jnp.concatenate([vp_ref[...], vc_ref[...]], axis=0).astype(F32)
    k_all_r = pltpu.roll(k_all, HEAD_DIM, axis=1)
    v_all_r = pltpu.roll(v_all, HEAD_DIM, axis=1)
    zero = jnp.zeros((nkeys, LANES), F32)
    rows_per_kv = GQA_GROUP // 2
    for qb in range(ATTN_Q_BLOCKS):
        qrows = slice(qb * ATTN_BLOCK, (qb + 1) * ATTN_BLOCK)
        krows = slice(qb * ATTN_BLOCK, qb * ATTN_BLOCK + nkeys)
        k2, k2r, v2, v2r = k_all[krows], k_all_r[krows], v_all[krows], v_all_r[krows]
        bias = bias_ref[jnp.minimum(n, 1)] if qb == 0 else bias_ref[1]
        for hh in range(2):
            kd = jnp.where(lo, k2, k2r) if hh == 0 else jnp.where(lo, k2r, k2)
            vd = jnp.where(lo, v2, v2r) if hh == 0 else jnp.where(lo, v2r, v2)
            kbd = jnp.concatenate([jnp.where(lo, kd, zero), jnp.where(lo, zero, kd)],
                                  axis=0).astype(BF16)
            vbd = jnp.concatenate([jnp.where(lo, vd, zero), jnp.where(lo, zero, vd)],
                                  axis=0).astype(BF16)
            base = hh * rows_per_kv * LANES
            qs = jnp.concatenate(
                [q_ref[qrows, base + r * LANES: base + (r + 1) * LANES]
                 for r in range(rows_per_kv)], axis=0)
            s = lax.dot_general(qs * jnp.asarray(HEAD_DIM ** -0.5, BF16), kbd,
                                (((1,), (1,)), ((), ())),
                                preferred_element_type=F32) + bias
            p_rows, inv_rows = [], []
            for r in range(rows_per_kv):
                halves, invs = [], []
                for c in range(2):
                    sink = sink_ref[(2 * hp + hh) * GQA_GROUP + 2 * r + c]
                    blk = s[r * ATTN_BLOCK:(r + 1) * ATTN_BLOCK, c * nkeys:(c + 1) * nkeys]
                    m = jnp.maximum(jnp.max(blk, axis=-1, keepdims=True), sink)
                    e = jnp.exp(blk - m)
                    denom = jnp.sum(e, axis=-1, keepdims=True) + jnp.exp(sink - m)
                    halves.append(e.astype(BF16))
                    invs.append(1.0 / denom)
                p_rows.append(jnp.concatenate(halves, axis=1))
                inv_rows.append(jnp.where(lo_q, invs[0], invs[1]))
            p = jnp.concatenate(p_rows, axis=0)
            o = jnp.dot(p, vbd, preferred_element_type=F32)
            for r in range(rows_per_kv):
                o_ref[qrows, base + r * LANES: base + (r + 1) * LANES] = (
                    o[r * ATTN_BLOCK:(r + 1) * ATTN_BLOCK] * inv_rows[r]).astype(o_ref.dtype)


def _attn_bias():
    nkeys = 2 * ATTN_BLOCK
    rows = (GQA_GROUP // 2) * ATTN_BLOCK
    qi = (jnp.arange(rows) % ATTN_BLOCK)[:, None]
    kj = (jnp.arange(2 * nkeys) % nkeys)[None, :]
    rel = kj - ATTN_BLOCK - qi
    in_band = (rel <= 0) & (rel > -ATTN_BLOCK)
    first = in_band & (kj >= ATTN_BLOCK)
    return jnp.where(jnp.stack([first, in_band]), 0.0, MASK_VALUE).astype(F32)


def sliding_window_attention(qkv, sinks, w_side, n_side_cols):
    s = qkv.shape[0]
    n_hp = N_KV_HEADS // 2
    q_rows = ATTN_Q_BLOCKS * ATTN_BLOCK
    assert s % q_rows == 0
    side_in, side_out, side_shape = _side_cast_specs(
        w_side, (s // q_rows) * n_hp, lambda n, hp: n * n_hp + hp, n_side_cols)
    q_width = N_KV_HEADS * GQA_GROUP * HEAD_DIM
    pair = 2 * GQA_GROUP * HEAD_DIM
    k_blk = q_width // LANES
    v_blk = k_blk + N_KV_HEADS * HEAD_DIM // LANES
    rows = (GQA_GROUP // 2) * ATTN_BLOCK
    prev = lambda n: jnp.maximum(n * ATTN_Q_BLOCKS - 1, 0)
    kv_prev, kv_cur = (ATTN_BLOCK, LANES), (q_rows, LANES)
    return pl.pallas_call(
        _attn_kernel,
        grid=(s // q_rows, n_hp),
        in_specs=[
            pl.BlockSpec(memory_space=pltpu.SMEM),
            pl.BlockSpec((q_rows, pair), lambda n, hp: (n, hp)),
            pl.BlockSpec(kv_prev, lambda n, hp: (prev(n), k_blk + hp)),
            pl.BlockSpec(kv_cur, lambda n, hp: (n, k_blk + hp)),
            pl.BlockSpec(kv_prev, lambda n, hp: (prev(n), v_blk + hp)),
            pl.BlockSpec(kv_cur, lambda n, hp: (n, v_blk + hp)),
            pl.BlockSpec((2, rows, 4 * ATTN_BLOCK), lambda n, hp: (0, 0, 0)),
            side_in,
        ],
        out_specs=[pl.BlockSpec((q_rows, pair), lambda n, hp: (n, hp)), side_out],
        out_shape=[jax.ShapeDtypeStruct((s, q_width), BF16), side_shape],
        compiler_params=_params(2),
        name="swa_attention",
    )(sinks, qkv, qkv, qkv, qkv, qkv, _attn_bias(), w_side)


def _pool_kernel(u_ref, halo_ref, wp_ref, sc_ref, o_ref, *, bm, gdim):
    i = pl.program_id(0)
    t = lax.broadcasted_iota(jnp.int32, (bm, 1), 0) + i * bm
    for g, w in enumerate(POOL_WINDOWS):
        cols = slice(g * gdim, (g + 1) * gdim)
        u = u_ref[:, cols]
        halo = jnp.where(i > 0, halo_ref[:, cols], 0.0)
        acc = jnp.concatenate([halo, u], axis=0)
        k = 1
        while k < w:
            acc = acc + pltpu.roll(acc, k, axis=0)
            k *= 2
        count = jnp.minimum(t + 1, w).astype(F32)
        mixed = (acc[POOL_HALO:] * (1.0 / count) - u).astype(BF16)
        y = jnp.dot(mixed, wp_ref[g], preferred_element_type=F32) * sc_ref[:, cols]
        o_ref[:, cols] = y.astype(o_ref.dtype)


def multiscale_pool(u, w_pool, pool_scale, *, bm=512):
    s, width = u.shape
    ngroups, gdim, _ = w_pool.shape
    return pl.pallas_call(
        functools.partial(_pool_kernel, bm=bm, gdim=gdim),
        grid=(s // bm,),
        in_specs=[
            pl.BlockSpec((bm, width), lambda i: (i, 0)),
            pl.BlockSpec((POOL_HALO, width),
                         lambda i: (jnp.maximum(i * (bm // POOL_HALO) - 1, 0), 0)),
            pl.BlockSpec((ngroups, gdim, gdim), lambda i: (0, 0, 0)),
            pl.BlockSpec((1, width), lambda i: (0, 0)),
        ],
        out_specs=pl.BlockSpec((bm, width), lambda i: (i, 0)),
        out_shape=jax.ShapeDtypeStruct((s, width), BF16),
        compiler_params=_params(1),
        name="multiscale_pool",
    )(u, u, w_pool, pool_scale.reshape(1, width))


def _merge_kernel(h_ref, a_ref, p_ref, wga_ref, wgp_ref, wba_ref, wbp_ref, s0_ref, s1_ref,
                  o_ref, d0_ref, d1_ref):
    h = h_ref[...]
    dot = functools.partial(jnp.dot, preferred_element_type=F32)
    ya = jax.nn.sigmoid(dot(h, wga_ref[...])) * dot(a_ref[...], wba_ref[...])
    yp = jax.nn.sigmoid(dot(h, wgp_ref[...])) * dot(p_ref[...], wbp_ref[...])
    o_ref[...] = (ya + yp).astype(o_ref.dtype)
    _side_cast(s0_ref, d0_ref)
    _side_cast(s1_ref, d1_ref)


def gated_merge(h, attn, pool, w_gates, w_ba, w_bp, side_weights, *, bm, bn):
    s, d = h.shape
    ka, kp = attn.shape[1], pool.shape[1]
    nj = d // bn
    assert d % bn == 0 and s % bm == 0
    side = [_side_cast_specs(sw, (s // bm) * nj, lambda i, j: i * nj + j) for sw in side_weights]
    return pl.pallas_call(
        _merge_kernel,
        grid=(s // bm, nj),
        in_specs=[
            pl.BlockSpec((bm, d), lambda i, j: (i, 0)),
            pl.BlockSpec((bm, ka), lambda i, j: (i, 0)),
            pl.BlockSpec((bm, kp), lambda i, j: (i, 0)),
            pl.BlockSpec((d, bn), lambda i, j: (0, j)),
            pl.BlockSpec((d, bn), lambda i, j: (0, j + nj)),
            pl.BlockSpec((ka, bn), lambda i, j: (0, j)),
            pl.BlockSpec((kp, bn), lambda i, j: (0, j)),
            side[0][0], side[1][0],
        ],
        out_specs=[pl.BlockSpec((bm, bn), lambda i, j: (i, j)), side[0][1], side[1][1]],
        out_shape=[jax.ShapeDtypeStruct((s, d), BF16), side[0][2], side[1][2]],
        compiler_params=_params(2),
        name="gated_merge",
    )(h, attn, pool, w_gates, w_gates, w_ba, w_bp, *side_weights)


def _ffn_up_kernel(h_ref, wg_ref, wv_ref, cwg_ref, cwv_ref, cbg_ref, cbv_ref, wsrc_ref,
                   o_ref, wdst_ref, w_ref, u_ref, *, bm, bn, n_sub):
    i = pl.program_id(1)
    _side_cast(wsrc_ref, wdst_ref)

    @pl.when(i == 0)
    def _():
        _stage_bf16(w_ref, 0, wg_ref)
        _stage_bf16(w_ref, bn, wv_ref)
        u_ref[...] = jnp.zeros(u_ref.shape, F32)

    cw = jnp.concatenate([cwg_ref[...], cwv_ref[...]], axis=1)
    cb = jnp.concatenate([cbg_ref[...], cbv_ref[...]], axis=1)
    sub = bm // n_sub
    prev = u_ref[...]
    ups = [jnp.dot(h_ref[r * sub:(r + 1) * sub, :], w_ref[...], preferred_element_type=F32)
           for r in range(n_sub)]
    for r in range(n_sub):
        lo = r * sub
        up = ups[r]
        ext = jnp.concatenate([prev, up], axis=0)
        y = cb + up * cw[CONV_WIDTH - 1:CONV_WIDTH, :]
        for back in range(1, CONV_WIDTH):
            y = y + (pltpu.roll(ext, back, axis=0)[SUBLANES:]
                     * cw[CONV_WIDTH - 1 - back:CONV_WIDTH - back, :])
        o_ref[lo:lo + sub, :] = (jax.nn.gelu(y[:, :bn], approximate=True)
                                 * y[:, bn:]).astype(o_ref.dtype)
        prev = up[sub - SUBLANES:]
    u_ref[...] = prev


def conv_geglu_up(h, w_up, conv_w, conv_b, w_side, *, bm, bn, n_sub):
    s, d = h.shape
    d_ff = w_up.shape[1] // 2
    nj = d_ff // bn
    ni = s // bm
    assert d_ff % bn == 0 and d % CAST_ROWS == 0 and bm % n_sub == 0
    side_in, side_out, side_shape = _side_cast_specs(w_side, nj * ni, lambda j, i: j * ni + i)
    return pl.pallas_call(
        functools.partial(_ffn_up_kernel, bm=bm, bn=bn, n_sub=n_sub),
        grid=(nj, s // bm),
        in_specs=[
            pl.BlockSpec((bm, d), lambda j, i: (i, 0)),
            pl.BlockSpec((d, bn), lambda j, i: (0, j)),
            pl.BlockSpec((d, bn), lambda j, i: (0, j + nj)),
            pl.BlockSpec((CONV_WIDTH, bn), lambda j, i: (0, j)),
            pl.BlockSpec((CONV_WIDTH, bn), lambda j, i: (0, j + nj)),
            pl.BlockSpec((1, bn), lambda j, i: (0, j)),
            pl.BlockSpec((1, bn), lambda j, i: (0, j + nj)),
            side_in,
        ],
        out_specs=[pl.BlockSpec((bm, bn), lambda j, i: (i, j)), side_out],
        out_shape=[jax.ShapeDtypeStruct((s, d_ff), BF16), side_shape],
        scratch_shapes=[pltpu.VMEM((d, 2 * bn), BF16),
                        pltpu.VMEM((SUBLANES, 2 * bn), F32)],
        compiler_params=_params(2),
        name="conv_geglu_up",
    )(h, w_up, w_up, conv_w, conv_w, conv_b.reshape(1, -1), conv_b.reshape(1, -1), w_side)


def _stash_tile(t, res_ref, xo_ref, r_ref, ss_ref, *, bn):
    j = pl.program_id(1)
    xo_ref[:, pl.ds(pl.multiple_of(j * bn, bn), bn)] = t
    r_ref[j] = res_ref[...]
    ss_prev = jnp.where(j == 0, 0.0, ss_ref[...])
    ss_ref[...] = ss_prev + jnp.sum(t * t, axis=-1, keepdims=True)


def _finish_rows(r_ref, ss_ref, gp_ref, gn_ref, xo_ref, ho_ref, *, bn):
    nj = r_ref.shape[0]
    inv_d = 1.0 / (nj * bn)
    scale = lax.rsqrt(ss_ref[...] * inv_d + RMS_EPS)
    ss_new = jnp.zeros_like(scale)
    for jj in range(nj):
        cols = slice(jj * bn, (jj + 1) * bn)
        xn = r_ref[jj] + xo_ref[:, cols] * scale * gp_ref[:, cols]
        xo_ref[:, cols] = xn
        ss_new = ss_new + jnp.sum(xn * xn, axis=-1, keepdims=True)
    if ho_ref is not None:
        scale_next = lax.rsqrt(ss_new * inv_d + RMS_EPS)
        for jj in range(nj):
            cols = slice(jj * bn, (jj + 1) * bn)
            ho_ref[:, cols] = (xo_ref[:, cols] * scale_next * gn_ref[:, cols]).astype(ho_ref.dtype)


def _mm_resnorm_kernel(a_ref, w_ref, res_ref, gp_ref, gn_ref, xo_ref, ho_ref,
                       r_ref, ss_ref, *, bn):
    t = jnp.dot(a_ref[...], w_ref[...], preferred_element_type=F32)
    _stash_tile(t, res_ref, xo_ref, r_ref, ss_ref, bn=bn)

    @pl.when(pl.program_id(1) == pl.num_programs(1) - 1)
    def _():
        _finish_rows(r_ref, ss_ref, gp_ref, gn_ref, xo_ref, ho_ref, bn=bn)


def matmul_residual_norm(a, w, resid, g_post, g_next, *, bm, bn):
    s, k = a.shape
    d = w.shape[1]
    nj = d // bn
    assert s % bm == 0 and d % bn == 0
    row = pl.BlockSpec((bm, d), lambda i, j: (i, 0))
    vec = pl.BlockSpec((1, d), lambda i, j: (0, 0))
    return pl.pallas_call(
        functools.partial(_mm_resnorm_kernel, bn=bn),
        grid=(s // bm, nj),
        in_specs=[pl.BlockSpec((bm, k), lambda i, j: (i, 0)),
                  pl.BlockSpec((k, bn), lambda i, j: (0, j)),
                  pl.BlockSpec((bm, bn), lambda i, j: (i, j)),
                  vec, vec],
        out_specs=[row, row],
        out_shape=[jax.ShapeDtypeStruct((s, d), F32), jax.ShapeDtypeStruct((s, d), BF16)],
        scratch_shapes=[pltpu.VMEM((nj, bm, bn), F32), pltpu.VMEM((bm, 1), F32)],
        compiler_params=_params(2),
        name="proj_out_norm",
    )(a, w, resid, g_post.reshape(1, d), g_next.reshape(1, d))


def _ple_resnorm_kernel(h_ref, p_ref, wg_ref, wp_ref, x_ref, tp_ref, sp_ref, gpp_ref, gp_ref,
                        xo_ref, t_ref, ss_ref, *, ni, nj, bn):
    i, j = pl.program_id(0), pl.program_id(1)
    cur, done = 0, 1

    @pl.when(i > 0)
    def _():
        resid = x_ref[...] + tp_ref[...] * sp_ref[...] * gpp_ref[...]
        scale = lax.rsqrt(ss_ref[done] * (1.0 / (nj * bn)) + RMS_EPS)
        xo_ref[...] = resid + t_ref[j] * scale * gp_ref[...]

    @pl.when(i < ni)
    def _():
        g = jnp.dot(h_ref[...], wg_ref[...], preferred_element_type=F32)
        e = jnp.dot(p_ref[...].astype(BF16), wp_ref[...], preferred_element_type=F32)
        t = jax.nn.sigmoid(g) * e
        t_ref[j] = t
        ss = jnp.where(j == 0, 0.0, ss_ref[cur]) + jnp.sum(t * t, axis=-1, keepdims=True)
        ss_ref[cur] = ss

        @pl.when(j == nj - 1)
        def _():
            ss_ref[done] = ss


def ple_residual_norm(h, p, w_gate, w_ple, x, t_prev, scale_prev, g_prev, g_post, *, bm, bn):
    s, d = h.shape
    pd = p.shape[1]
    ni, nj = s // bm, d // bn
    assert s % bm == 0 and d % bn == 0
    mm_row = lambda i, j: (jnp.minimum(i, ni - 1), 0)
    out_tile = lambda i, j: (jnp.maximum(i - 1, 0), jnp.where(i == 0, 0, j))
    return pl.pallas_call(
        functools.partial(_ple_resnorm_kernel, ni=ni, nj=nj, bn=bn),
        grid=(ni + 1, nj),
        in_specs=[
            pl.BlockSpec((bm, d), mm_row),
            pl.BlockSpec((bm, pd), mm_row),
            pl.BlockSpec((d, bn), lambda i, j: (0, j)),
            pl.BlockSpec((pd, bn), lambda i, j: (0, j)),
            pl.BlockSpec((bm, bn), out_tile),
            pl.BlockSpec((bm, bn), out_tile),
            pl.BlockSpec((bm, 1), lambda i, j: (jnp.maximum(i - 1, 0), 0)),
            pl.BlockSpec((1, bn), lambda i, j: (0, j)),
            pl.BlockSpec((1, bn), lambda i, j: (0, j)),
        ],
        out_specs=pl.BlockSpec((bm, bn), out_tile),
        out_shape=jax.ShapeDtypeStruct((s, d), F32),
        scratch_shapes=[pltpu.VMEM((nj, bm, bn), F32), pltpu.VMEM((2, bm, 1), F32)],
        compiler_params=_params(2),
        name="ple_norm",
    )(h, p, w_gate, w_ple, x, t_prev, scale_prev, g_prev.reshape(1, d), g_post.reshape(1, d))


def _layer(x, p, norm_mix_pre, w_in, attn_sinks, w_pool, pool_scale, w_branch_attn,
           w_branch_pool, w_out, norm_mix_post, norm_ffn_pre, w_up, conv_w, conv_b,
           w_down, norm_ffn_post, norm_ple_gate, w_ple_gate, w_ple, norm_ple_post):
    d = x.shape[1]
    attn_width = w_branch_attn.shape[0]
    pool_width = w_branch_pool.shape[0]
    qkv_width = attn_width + 2 * N_KV_HEADS * HEAD_DIM
    h = rmsnorm_cast(x, norm_mix_pre)
    qkv = matmul_wcast(h, w_in, n_out=qkv_width, col_off=0, bm=1024, bn=512,
                       out_dtype=BF16, name="proj_qkv")
    u, w_ba, w_bp = matmul_wcast(h, w_in, n_out=pool_width, col_off=qkv_width, bm=1024, bn=512,
                                 out_dtype=F32, name="proj_pool_in",
                                 side_weights=(w_branch_attn, w_branch_pool))
    attn, w_gates = sliding_window_attention(qkv, attn_sinks, w_in, 2 * d)
    pool = multiscale_pool(u, w_pool.astype(BF16), pool_scale)
    merged, w_out_b, w_ple_gate_b = gated_merge(h, attn, pool, w_gates, w_ba, w_bp,
                                                (w_out, w_ple_gate), bm=1024, bn=256)
    x, h = matmul_residual_norm(merged, w_out_b, x, norm_mix_post, norm_ffn_pre,
                                bm=512, bn=512)

    act, w_down_b = conv_geglu_up(h, w_up, conv_w, conv_b, w_down, bm=2048, bn=256, n_sub=4)
    t = matmul(act, w_down_b, n_out=d, col_off=0, bm=512, bn=512,
               out_dtype=F32, name="proj_down")
    h, t_scale = residual_norm(x, t, norm_ffn_post, norm_ple_gate)

    return ple_residual_norm(h, p, w_ple_gate_b, w_ple.astype(BF16), x, t, t_scale,
                             norm_ffn_post, norm_ple_post, bm=1024, bn=512)


def kernel(x, p, norm_mix_pre, w_in, attn_sinks, w_pool, pool_scale, w_branch_attn,
           w_branch_pool, w_out, norm_mix_post, norm_ffn_pre, w_up, conv_w, conv_b,
           w_down, norm_ffn_post, norm_ple_gate, w_ple_gate, w_ple, norm_ple_post):
    batch, seq, d = x.shape
    depth = w_in.shape[0]
    assert batch == 1
    xs = x.reshape(seq, d)
    for i in range(depth):
        xs = _layer(xs, p[i, 0], norm_mix_pre[i], w_in[i], attn_sinks[i], w_pool[i],
                    pool_scale[i], w_branch_attn[i], w_branch_pool[i], w_out[i],
                    norm_mix_post[i], norm_ffn_pre[i], w_up[i], conv_w[i], conv_b[i],
                    w_down[i], norm_ffn_post[i], norm_ple_gate[i], w_ple_gate[i],
                    w_ple[i], norm_ple_post[i])
    return xs.reshape(batch, seq, d)
```
